```python
import math
import jax, jax.numpy as jnp
from jax import lax
import numpy as np


D_MODEL = 1024
BATCH = 2
SEQ = 8192
DEPTH = 1
DEC_BATCH = 128
DEC_SEQ = 8
PAST_LEN = 8192
PAGE_SIZE = 128

HEAD_DIM = 64
SB_HEADS = 8
DIFF_HEADS = 4
SB_WIDTH = SB_HEADS * HEAD_DIM
DIFF_QK_WIDTH = DIFF_HEADS * 2 * HEAD_DIM
DIFF_V_DIM = 2 * HEAD_DIM
DIFF_WIDTH = DIFF_HEADS * DIFF_V_DIM
MIX_WIDTH = SB_WIDTH + DIFF_WIDTH
IN_WIDTH = 3 * SB_WIDTH + 2 * DIFF_QK_WIDTH + DIFF_WIDTH
IN_SPLITS = (SB_WIDTH, 2 * SB_WIDTH, 3 * SB_WIDTH,
             3 * SB_WIDTH + DIFF_QK_WIDTH, 3 * SB_WIDTH + 2 * DIFF_QK_WIDTH)
D_FF = ((8 * D_MODEL + 3 * 256 - 1) // (3 * 256)) * 256
ROPE_THETA = 10000.0
Q_BLOCK = 128
LN_EPS = 1e-5
DEEPNORM_ALPHA = (2 * DEPTH) ** 0.25
DEEPNORM_BETA = (8 * DEPTH) ** -0.25
N_MOD = 6

kernel_name = "hybrid_stickbreak_diffattn_deepnorm_step"


def lambda_init_value(layer):
    return 0.8 - 0.6 * math.exp(-0.3 * layer)


def layer_norm(x, g, b):
    xf = x.astype(jnp.float32)
    mu = jnp.mean(xf, axis=-1, keepdims=True)
    var = jnp.mean(jnp.square(xf - mu), axis=-1, keepdims=True)
    return ((xf - mu) * lax.rsqrt(var + LN_EPS)).astype(x.dtype) * g + b


def rms_norm(x, g):
    xf = x.astype(jnp.float32)
    return (xf * lax.rsqrt(jnp.mean(jnp.square(xf), axis=-1, keepdims=True) + LN_EPS)).astype(x.dtype) * g


def rope(x, pos):
    half = HEAD_DIM // 2
    inv = ROPE_THETA ** (-jnp.arange(half, dtype=jnp.float32) / half)
    ang = pos.astype(jnp.float32)[:, None] * inv[None, :]
    cos = jnp.cos(ang)[:, None, :]
    sin = jnp.sin(ang)[:, None, :]
    xf = x.astype(jnp.float32)
    x1, x2 = xf[..., :half], xf[..., half:]
    return jnp.concatenate([x1 * cos - x2 * sin, x2 * cos + x1 * sin], axis=-1).astype(x.dtype)


def ada_mod(c, w, b):
    return jnp.split(jax.nn.silu(c) @ w + b, N_MOD, axis=-1)


def modulate(x, shift, scale):
    return x * (1.0 + scale[:, None, :]) + shift[:, None, :]


def mixer_inputs(h, w):
    lead = h.shape[:-1]
    sbq, sbk, sbv, dq, dk, dv = jnp.split(h @ w, IN_SPLITS, axis=-1)
    return (sbq.reshape(*lead, SB_HEADS, HEAD_DIM),
            sbk.reshape(*lead, SB_HEADS, HEAD_DIM),
            sbv.reshape(*lead, SB_HEADS, HEAD_DIM),
            dq.reshape(*lead, 2 * DIFF_HEADS, HEAD_DIM),
            dk.reshape(*lead, 2 * DIFF_HEADS, HEAD_DIM),
            dv.reshape(*lead, DIFF_HEADS, DIFF_V_DIM))


def stick_breaking(q, k, v, qpos, kpos):
    z = jnp.einsum('bqhd,bkhd->bhqk', q, k).astype(jnp.float32) * (HEAD_DIM ** -0.5)
    mask = kpos[None, :] < qpos[:, None]
    log_1m = jnp.where(mask, jax.nn.log_sigmoid(-z), 0.0)
    tail = lax.cumsum(log_1m, axis=3, reverse=True) - log_1m
    w = jnp.where(mask, jnp.exp(jax.nn.log_sigmoid(z) + tail), 0.0)
    return jnp.einsum('bhqk,bkhd->bqhd', w.astype(v.dtype), v)


def diff_attention(q, k, v, qpos, kpos, lam):
    b, tq = q.shape[0], q.shape[1]
    tk = k.shape[1]
    s = jnp.einsum('bqhd,bkhd->bhqk', q, k).astype(jnp.float32) * (HEAD_DIM ** -0.5)
    s = jnp.where(kpos[None, :] <= qpos[:, None], s, -jnp.inf)
    p = jax.nn.softmax(s, axis=-1).reshape(b, DIFF_HEADS, 2, tq, tk)
    a = p[:, :, 0] - lam * p[:, :, 1]
    return jnp.einsum('bhqk,bkhe->bqhe', a.astype(v.dtype), v)


def sweep_query_blocks(attn, q, qpos):
    b, s = q.shape[0], q.shape[1]
    nb = s // Q_BLOCK
    qb = jnp.moveaxis(q.reshape(b, nb, Q_BLOCK, *q.shape[2:]), 1, 0)
    pb = qpos.reshape(nb, Q_BLOCK)
    o = lax.map(lambda a: attn(a[0], a[1]), (qb, pb))
    return jnp.moveaxis(o, 0, 1).reshape(b, s, *o.shape[3:])


def merge_mixers(x, sb_o, d_o, gate, w_o, g_sub, lam_init, g, b):
    lead = x.shape[:-1]
    d_o = rms_norm(d_o, g_sub) * (1.0 - lam_init)
    o = jnp.concatenate([sb_o.reshape(*lead, SB_WIDTH), d_o.reshape(*lead, DIFF_WIDTH)], axis=-1) @ w_o
    return layer_norm(DEEPNORM_ALPHA * x + gate[:, None, :] * o, g, b)


def swiglu_block(x, shift, scale, gate, w_gu, w_dn, g, b):
    gt, up = jnp.split(modulate(x, shift, scale) @ w_gu, 2, axis=-1)
    f = (jax.nn.silu(gt) * up) @ w_dn
    return layer_norm(DEEPNORM_ALPHA * x + gate[:, None, :] * f, g, b)


def setup_inputs(seed: int = 0) -> dict:
    key = jax.random.key(seed)
    ks = jax.random.split(key, 32)
    n_pages = PAST_LEN // PAGE_SIZE
    n_pool = (5 * DEC_BATCH * n_pages + 3) // 4
    nrm = jax.random.normal
    f32 = jnp.float32
    page_table = jax.random.permutation(ks[0], n_pool)[:DEC_BATCH * n_pages].reshape(DEC_BATCH, n_pages).astype(jnp.int32)
    return {
        'x_prompt': nrm(ks[1], (BATCH, SEQ, D_MODEL), f32),
        'x_sample': nrm(ks[2], (DEC_BATCH, DEC_SEQ, D_MODEL), f32),
        'cache_sb_k': nrm(ks[3], (n_pool, DEPTH, PAGE_SIZE, SB_HEADS, HEAD_DIM), f32),
        'cache_sb_v': nrm(ks[4], (n_pool, DEPTH, PAGE_SIZE, SB_HEADS, HEAD_DIM), f32),
        'cache_diff_k': nrm(ks[5], (n_pool, DEPTH, PAGE_SIZE, 2 * DIFF_HEADS, HEAD_DIM), f32),
        'cache_diff_v': nrm(ks[6], (n_pool, DEPTH, PAGE_SIZE, DIFF_HEADS, DIFF_V_DIM), f32),
        'page_table': page_table,
        'c_prompt': nrm(ks[7], (BATCH, D_MODEL), f32),
        'c_sample': nrm(ks[8], (DEC_BATCH, D_MODEL), f32),
        'w_ada': nrm(ks[9], (DEPTH, D_MODEL, N_MOD * D_MODEL), f32) * (0.5 * D_MODEL ** -0.5),
        'b_ada': nrm(ks[10], (DEPTH, N_MOD * D_MODEL), f32) * 0.02,
        'w_in': nrm(ks[11], (DEPTH, D_MODEL, IN_WIDTH), f32) * (D_MODEL ** -0.5),
        'w_out': nrm(ks[12], (DEPTH, MIX_WIDTH, D_MODEL), f32) * (MIX_WIDTH ** -0.5 * DEEPNORM_BETA),
        'lambda_q1': nrm(ks[13], (DEPTH, HEAD_DIM), f32) * 0.1,
        'lambda_k1': nrm(ks[14], (DEPTH, HEAD_DIM), f32) * 0.1,
        'lambda_q2': nrm(ks[15], (DEPTH, HEAD_DIM), f32) * 0.1,
        'lambda_k2': nrm(ks[16], (DEPTH, HEAD_DIM), f32) * 0.1,
        'subln_g': 1.0 + 0.02 * nrm(ks[17], (DEPTH, DIFF_V_DIM), f32),
        'ln1_g': 1.0 + 0.02 * nrm(ks[18], (DEPTH, D_MODEL), f32),
        'ln1_b': 0.02 * nrm(ks[19], (DEPTH, D_MODEL), f32),
        'w_gate_up': nrm(ks[20], (DEPTH, D_MODEL, 2 * D_FF), f32) * (D_MODEL ** -0.5),
        'w_down': nrm(ks[21], (DEPTH, D_FF, D_MODEL), f32) * (D_FF ** -0.5 * DEEPNORM_BETA),
        'ln2_g': 1.0 + 0.02 * nrm(ks[22], (DEPTH, D_MODEL), f32),
        'ln2_b': 0.02 * nrm(ks[23], (DEPTH, D_MODEL), f32),
    }


def reference(x_prompt, x_sample, cache_sb_k, cache_sb_v, cache_diff_k, cache_diff_v, page_table,
              c_prompt, c_sample, w_ada, b_ada, w_in, w_out, lambda_q1, lambda_k1, lambda_q2, lambda_k2,
              subln_g, ln1_g, ln1_b, w_gate_up, w_down, ln2_g, ln2_b):
    n_new = x_sample.shape[1]
    pos_p = jnp.arange(x_prompt.shape[1], dtype=jnp.int32)
    qpos_s = PAST_LEN + jnp.arange(n_new, dtype=jnp.int32)
    kpos_s = jnp.arange(PAST_LEN + n_new, dtype=jnp.int32)
    xp, xs = x_prompt, x_sample
    new_p = ([], [], [], [])
    new_s = ([], [], [], [])
    for l in range(DEPTH):
        lam_init = lambda_init_value(l)
        lam = (jnp.exp(jnp.sum(lambda_q1[l].astype(jnp.float32) * lambda_k1[l].astype(jnp.float32)))
               - jnp.exp(jnp.sum(lambda_q2[l].astype(jnp.float32) * lambda_k2[l].astype(jnp.float32)))
               + lam_init)
        mp = ada_mod(c_prompt, w_ada[l], b_ada[l])
        ms = ada_mod(c_sample, w_ada[l], b_ada[l])

        sbq, sbk, sbv, dq, dk, dv = mixer_inputs(modulate(xp, mp[0], mp[1]), w_in[l])
        dq, dk = rope(dq, pos_p), rope(dk, pos_p)
        sb_o = sweep_query_blocks(lambda qb, pb: stick_breaking(qb, sbk, sbv, pb, pos_p), sbq, pos_p)
        d_o = sweep_query_blocks(lambda qb, pb: diff_attention(qb, dk, dv, pb, pos_p, lam), dq, pos_p)
        xp = merge_mixers(xp, sb_o, d_o, mp[2], w_out[l], subln_g[l], lam_init, ln1_g[l], ln1_b[l])
        xp = swiglu_block(xp, mp[3], mp[4], mp[5], w_gate_up[l], w_down[l], ln2_g[l], ln2_b[l])
        for lst, a in zip(new_p, (sbk, sbv, dk, dv)):
            lst.append(a)

        sbq_s, sbk_s, sbv_s, dq_s, dk_s, dv_s = mixer_inputs(modulate(xs, ms[0], ms[1]), w_in[l])
        dq_s, dk_s = rope(dq_s, qpos_s), rope(dk_s, qpos_s)

        def attend_one(args, l=l, lam=lam):
            sq, sk, sv, q2, k2, v2, prow = args

            def with_past(cache, new):
                past = cache[prow, l]
                past = past.reshape(-1, *past.shape[-2:])
                return jnp.concatenate([past, new], axis=0)[None]

            sb = stick_breaking(sq[None], with_past(cache_sb_k, sk), with_past(cache_sb_v, sv), qpos_s, kpos_s)[0]
            d = diff_attention(q2[None], with_past(cache_diff_k, k2), with_past(cache_diff_v, v2), qpos_s, kpos_s, lam)[0]
            return sb, d

        sb_os, d_os = lax.map(attend_one, (sbq_s, sbk_s, sbv_s, dq_s, dk_s, dv_s, page_table))
        xs = merge_mixers(xs, sb_os, d_os, ms[2], w_out[l], subln_g[l], lam_init, ln1_g[l], ln1_b[l])
        xs = swiglu_block(xs, ms[3], ms[4], ms[5], w_gate_up[l], w_down[l], ln2_g[l], ln2_b[l])
        for lst, a in zip(new_s, (sbk_s, sbv_s, dk_s, dv_s)):
            lst.append(a)

    return (xp, xs,
            jnp.stack(new_p[0], axis=1), jnp.stack(new_p[1], axis=1),
            jnp.stack(new_p[2], axis=1), jnp.stack(new_p[3], axis=1),
            jnp.stack(new_s[0], axis=1), jnp.stack(new_s[1], axis=1),
            jnp.stack(new_s[2], axis=1), jnp.stack(new_s[3], axis=1))
```

```python
import functools
import math

import numpy as np
import jax
import jax.numpy as jnp
from jax import lax
from jax.experimental import pallas as pl
from jax.experimental.pallas import tpu as pltpu

F32 = jnp.float32
BF16 = jnp.bfloat16

D_MODEL = 1024
HEAD_DIM = 64
SB_HEADS = 8
DIFF_HEADS = 4
SB_WIDTH = SB_HEADS * HEAD_DIM
DIFF_V_DIM = 2 * HEAD_DIM
IN_WIDTH = 6 * SB_WIDTH
D_FF = 2816
PAGE_SIZE = 128
PAST_LEN = 8192
ROPE_THETA = 10000.0
LN_EPS = 1e-5
DEPTH = 1
DEEPNORM_ALPHA = (2 * DEPTH) ** 0.25
LAM_INIT = 0.8 - 0.6 * math.exp(-0.3 * 0)
QK_SCALE = HEAD_DIM ** -0.5
NEG = -1e30

LANES = 128
VMEM_LIMIT = 56 * 1024 * 1024

COL_SBQ, COL_SBK, COL_SBV, COL_DQ, COL_DK, COL_DV = 0, 4, 8, 12, 16, 20

NT_DIMS = (((1,), (1,)), ((), ()))


def _dot(a, b):
    return jnp.dot(a, b, preferred_element_type=F32)


def _dot_nt(a, b):
    return lax.dot_general(a, b, NT_DIMS, preferred_element_type=F32)


def _params(sem):
    return pltpu.CompilerParams(dimension_semantics=sem, vmem_limit_bytes=VMEM_LIMIT)


def _ada_kernel(c_ref, w_ref, b_ref, o_ref):
    c = c_ref[...]
    a = c * (1.0 / (1.0 + jnp.exp(-c)))
    o_ref[...] = jnp.dot(a, w_ref[...], preferred_element_type=F32,
                         precision=lax.Precision.HIGHEST) + b_ref[...]


def _ada(c_all, w, b):
    rows = c_all.shape[0]
    n = w.shape[1]
    tn = 1024
    return pl.pallas_call(
        _ada_kernel,
        grid=(n // tn,),
        in_specs=[pl.BlockSpec((rows, D_MODEL), lambda k: (0, 0)),
                  pl.BlockSpec((D_MODEL, tn), lambda k: (0, k)),
                  pl.BlockSpec((1, tn), lambda k: (0, k))],
        out_specs=pl.BlockSpec((rows, tn), lambda k: (0, k)),
        out_shape=jax.ShapeDtypeStruct((rows, n), F32),
        compiler_params=_params(("arbitrary",)),
        name="ada",
    )(c_all, w, b)


def _inproj_kernel(x_ref, shift_ref, scale_ref, w_ref, cos_ref, sin_ref,
                   qkv_ref, ksb_ref, vsb_ref, kd_ref, vd_ref):
    x = x_ref[0]
    h = x * (1.0 + scale_ref[0]) + shift_ref[0]
    y = _dot(h.astype(BF16), w_ref[...])
    tm = y.shape[0]
    cos = cos_ref[...]
    sin = sin_ref[...]
    lane = lax.broadcasted_iota(jnp.int32, (tm, LANES), 1)
    first_half = (lane & (HEAD_DIM - 1)) < (HEAD_DIM // 2)

    def rope(t):
        rot = jnp.where(first_half, pltpu.roll(t, LANES - HEAD_DIM // 2, 1),
                        pltpu.roll(t, HEAD_DIM // 2, 1))
        return t * cos + rot * sin

    w = SB_WIDTH
    qkv_ref[0, :, 0:w] = (y[:, 0:w] * QK_SCALE).astype(BF16)
    ksb = y[:, w:2 * w]
    vsb = y[:, 2 * w:3 * w]
    ksb_ref[0] = ksb
    vsb_ref[0] = vsb
    qkv_ref[0, :, w:2 * w] = ksb.astype(BF16)
    qkv_ref[0, :, 2 * w:3 * w] = vsb.astype(BF16)
    for g in range(w // LANES):
        lo = 3 * w + g * LANES
        qkv_ref[0, :, lo:lo + LANES] = (rope(y[:, lo:lo + LANES]) * QK_SCALE).astype(BF16)
        lo = 4 * w + g * LANES
        kd = rope(y[:, lo:lo + LANES])
        kd_ref[0, :, g * LANES:(g + 1) * LANES] = kd
        qkv_ref[0, :, lo:lo + LANES] = kd.astype(BF16)
    vd = y[:, 5 * w:6 * w]
    vd_ref[0] = vd
    qkv_ref[0, :, 5 * w:6 * w] = vd.astype(BF16)


def _mod_spec(mod, tm, col):
    if mod.shape[1] == 1:
        return pl.BlockSpec((1, 1, D_MODEL), lambda b, i: (b, 0, col))
    return pl.BlockSpec((1, tm, D_MODEL), lambda b, i: (b, i, col))


def _inproj(x, mod, w_in, cos, sin, tm):
    g, rows, _ = x.shape
    row_spec = lambda width: pl.BlockSpec((1, tm, width), lambda b, i: (b, i, 0))
    return pl.pallas_call(
        _inproj_kernel,
        grid=(g, rows // tm),
        in_specs=[row_spec(D_MODEL), _mod_spec(mod, tm, 0), _mod_spec(mod, tm, 1),
                  pl.BlockSpec((D_MODEL, IN_WIDTH), lambda b, i: (0, 0)),
                  pl.BlockSpec((tm, LANES), lambda b, i: (i, 0)),
                  pl.BlockSpec((tm, LANES), lambda b, i: (i, 0))],
        out_specs=[row_spec(IN_WIDTH)] + [row_spec(SB_WIDTH)] * 4,
        out_shape=[jax.ShapeDtypeStruct((g, rows, IN_WIDTH), BF16)]
                  + [jax.ShapeDtypeStruct((g, rows, SB_WIDTH), F32)] * 4,
        compiler_params=_params(("arbitrary", "arbitrary")),
        name="inproj",
    )(x, mod, mod, w_in, cos, sin)


def _tri_tables(nq, descending):
    qs, ks = [], []
    for qi in range(nq):
        order = range(qi, -1, -1) if descending else range(qi + 1)
        for kj in order:
            qs.append(qi)
            ks.append(kj)
    return jnp.asarray(np.array(qs, np.int32)), jnp.asarray(np.array(ks, np.int32))


def _sb_attn_kernel(qi_ref, kj_ref, q_ref, k_ref, v_ref, u_ref, o_ref, acc_ref, carry_ref):
    t = pl.program_id(2)
    qi = qi_ref[t]
    kj = kj_ref[t]
    tq = q_ref.shape[1]

    @pl.when(kj == qi)
    def _():
        acc_ref[...] = jnp.zeros_like(acc_ref)
        carry_ref[...] = jnp.zeros_like(carry_ref)

    def step(diag):
        q = q_ref[0]
        k = k_ref[0]
        v = v_ref[0]
        if diag:
            row = lax.broadcasted_iota(jnp.int32, (tq, tq), 0)
            col = lax.broadcasted_iota(jnp.int32, (tq, tq), 1)
            mask = col < row
        for h in range(2):
            sl = slice(h * HEAD_DIM, (h + 1) * HEAD_DIM)
            z = _dot_nt(q[:, sl], k[:, sl])
            sp = jnp.log(1.0 + jnp.exp(-jnp.abs(z)))
            l = jnp.minimum(-z, 0.0) - sp
            ls = l + z
            if diag:
                l = jnp.where(mask, l, 0.0)
            l_hi = l.astype(BF16)
            l_lo = (l - l_hi.astype(F32)).astype(BF16)
            tail = _dot(jnp.concatenate([l_hi, l_lo], axis=1), u_ref[...])
            carry = carry_ref[h]
            w = jnp.exp(ls + tail + carry)
            if diag:
                w = jnp.where(mask, w, 0.0)
            acc_ref[h] += _dot(w.astype(BF16), v[:, sl])
            carry_ref[h] = carry + tail[:, 0:1] + l[:, 0:1]

    pl.when(kj == qi)(functools.partial(step, True))
    pl.when(kj != qi)(functools.partial(step, False))

    @pl.when(kj == 0)
    def _():
        o_ref[0] = jnp.concatenate([acc_ref[0], acc_ref[1]], axis=1).astype(o_ref.dtype)


def _sb_attn(qkv, tq):
    b, s, _ = qkv.shape
    nq = s // tq
    qi_tab, kj_tab = _tri_tables(nq, descending=True)
    j = np.arange(tq)
    tri = (j[:, None] > j[None, :]).astype(np.float32)
    u2 = jnp.asarray(np.concatenate([tri, tri], axis=0), BF16)
    grid_spec = pltpu.PrefetchScalarGridSpec(
        num_scalar_prefetch=2,
        grid=(b, SB_HEADS // 2, int(qi_tab.shape[0])),
        in_specs=[pl.BlockSpec((1, tq, LANES), lambda bb, hp, t, qt, kt: (bb, qt[t], COL_SBQ + hp)),
                  pl.BlockSpec((1, tq, LANES), lambda bb, hp, t, qt, kt: (bb, kt[t], COL_SBK + hp)),
                  pl.BlockSpec((1, tq, LANES), lambda bb, hp, t, qt, kt: (bb, kt[t], COL_SBV + hp)),
                  pl.BlockSpec((2 * tq, tq), lambda bb, hp, t, qt, kt: (0, 0))],
        out_specs=pl.BlockSpec((1, tq, LANES), lambda bb, hp, t, qt, kt: (bb, qt[t], hp)),
        scratch_shapes=[pltpu.VMEM((2, tq, HEAD_DIM), F32), pltpu.VMEM((2, tq, 1), F32)],
    )
    return pl.pallas_call(
        _sb_attn_kernel,
        grid_spec=grid_spec,
        out_shape=jax.ShapeDtypeStruct((b, s, SB_WIDTH), BF16),
        compiler_params=_params(("arbitrary", "arbitrary", "arbitrary")),
        name="sb_attn",
    )(qi_tab, kj_tab, qkv, qkv, qkv, u2)


def _lambda_value(lq1, lk1, lq2, lk2):
    return (jnp.exp(jnp.sum(lq1 * lk1, axis=1, keepdims=True))
            - jnp.exp(jnp.sum(lq2 * lk2, axis=1, keepdims=True)) + LAM_INIT)


def _sub_rms_norm(o, g):
    ms = jnp.mean(o * o, axis=1, keepdims=True)
    return o * lax.rsqrt(ms + LN_EPS) * g * (1.0 - LAM_INIT)


def _diff_attn_kernel(qi_ref, kj_ref, q_ref, k_ref, v_ref, lq1_ref, lk1_ref, lq2_ref, lk2_ref, g_ref,
                      o_ref, acc_ref, m_ref, l_ref):
    t = pl.program_id(2)
    qi = qi_ref[t]
    kj = kj_ref[t]
    tq = q_ref.shape[1]

    @pl.when(kj == 0)
    def _():
        acc_ref[...] = jnp.zeros_like(acc_ref)
        l_ref[...] = jnp.zeros_like(l_ref)
        m_ref[...] = jnp.full_like(m_ref, NEG)

    def step(diag):
        q = q_ref[0]
        k = k_ref[0]
        v = v_ref[0]
        if diag:
            row = lax.broadcasted_iota(jnp.int32, (tq, tq), 0)
            col = lax.broadcasted_iota(jnp.int32, (tq, tq), 1)
            mask = col <= row
        for c in range(2):
            sl = slice(c * HEAD_DIM, (c + 1) * HEAD_DIM)
            s = _dot_nt(q[:, sl], k[:, sl])
            if diag:
                s = jnp.where(mask, s, NEG)
            m_prev = m_ref[c]
            m_new = jnp.maximum(m_prev, jnp.max(s, axis=1, keepdims=True))
            alpha = jnp.exp(m_prev - m_new)
            p = jnp.exp(s - m_new)
            l_ref[c] = alpha * l_ref[c] + jnp.sum(p, axis=1, keepdims=True)
            acc_ref[c] = alpha * acc_ref[c] + _dot(p.astype(BF16), v)
            m_ref[c] = m_new

    pl.when(kj == qi)(functools.partial(step, True))
    pl.when(kj != qi)(functools.partial(step, False))

    @pl.when(kj == qi)
    def _():
        lam = _lambda_value(lq1_ref[...], lk1_ref[...], lq2_ref[...], lk2_ref[...])
        o = acc_ref[0] / l_ref[0] - lam * (acc_ref[1] / l_ref[1])
        o_ref[0] = _sub_rms_norm(o, g_ref[...]).astype(o_ref.dtype)


def _diff_attn(qkv, lams, subln_g, tq):
    b, s, _ = qkv.shape
    nq = s // tq
    qi_tab, kj_tab = _tri_tables(nq, descending=False)
    small = lambda width: pl.BlockSpec((1, width), lambda bb, hv, t, qt, kt: (0, 0))
    grid_spec = pltpu.PrefetchScalarGridSpec(
        num_scalar_prefetch=2,
        grid=(b, DIFF_HEADS, int(qi_tab.shape[0])),
        in_specs=[pl.BlockSpec((1, tq, LANES), lambda bb, hv, t, qt, kt: (bb, qt[t], COL_DQ + hv)),
                  pl.BlockSpec((1, tq, LANES), lambda bb, hv, t, qt, kt: (bb, kt[t], COL_DK + hv)),
                  pl.BlockSpec((1, tq, LANES), lambda bb, hv, t, qt, kt: (bb, kt[t], COL_DV + hv)),
                  small(HEAD_DIM), small(HEAD_DIM), small(HEAD_DIM), small(HEAD_DIM), small(DIFF_V_DIM)],
        out_specs=pl.BlockSpec((1, tq, LANES), lambda bb, hv, t, qt, kt: (bb, qt[t], hv)),
        scratch_shapes=[pltpu.VMEM((2, tq, DIFF_V_DIM), F32), pltpu.VMEM((2, tq, 1), F32),
                        pltpu.VMEM((2, tq, 1), F32)],
    )
    return pl.pallas_call(
        _diff_attn_kernel,
        grid_spec=grid_spec,
        out_shape=jax.ShapeDtypeStruct((b, s, SB_WIDTH), BF16),
        compiler_params=_params(("arbitrary", "arbitrary", "arbitrary")),
        name="diff_attn",
    )(qi_tab, kj_tab, qkv, qkv, qkv, *lams, subln_g)


N_CLS = 8
DIFF_ROW_CLS = tuple(2 * hv + c for c in range(2) for hv in range(DIFF_HEADS))
SB_ROW_CLS = tuple(range(SB_HEADS))


def _lane_cls(width):
    return lax.broadcasted_iota(jnp.int32, (N_CLS, width), 1) & (N_CLS - 1)


def _pick_cls(full, row_cls, cls):
    out = None
    for rb, c in enumerate(row_cls):
        part = jnp.where(cls == c, full[rb * 8:(rb + 1) * 8], 0.0)
        out = part if out is None else out + part
    return out


def _spread_cls(x, row_cls, cls):
    return jnp.concatenate([jnp.where(cls == c, x, 0.0) for c in row_cls], axis=0)


def _shift_tokens(x, n):
    width = x.shape[1]
    if n % LANES == 0:
        return jnp.concatenate([x[:, n:], jnp.zeros((x.shape[0], n), x.dtype)], axis=1)
    lane = lax.broadcasted_iota(jnp.int32, x.shape, 1)
    return jnp.where(lane < width - n, pltpu.roll(x, width - n, 1), 0.0)


def _fold_chunks(x, op):
    out = x[:, 0:LANES]
    for c in range(1, x.shape[1] // LANES):
        out = op(out, x[:, c * LANES:(c + 1) * LANES])
    return out


def _allreduce_cls(x, op):
    n = N_CLS
    while n < LANES:
        x = op(x, pltpu.roll(x, n, 1))
        n *= 2
    return x


def _tile_lanes(x, width):
    return x if width == LANES else jnp.concatenate([x] * (width // LANES), axis=1)


def _sb_page(q, kp, vp, carry, acc, valid):
    width = kp.shape[0]
    cls = _lane_cls(width)
    z = _pick_cls(_dot_nt(q, kp.astype(BF16)), SB_ROW_CLS, cls)
    sp = jnp.log(1.0 + jnp.exp(-jnp.abs(z)))
    l = jnp.minimum(-z, 0.0) - sp
    ls = l + z
    if valid is not None:
        l = jnp.where(valid, l, 0.0)
    run = l
    n = N_CLS
    while n < width:
        run = run + _shift_tokens(run, n)
        n *= 2
    w = jnp.exp(ls + (run - l) + _tile_lanes(carry, width))
    if valid is not None:
        w = jnp.where(valid, w, 0.0)
    acc = acc + _dot(_spread_cls(w, SB_ROW_CLS, cls).astype(BF16), vp.astype(BF16))
    carry = carry + _allreduce_cls(_fold_chunks(l, jnp.add), jnp.add)
    return carry, acc


def _diff_page(q, kp, v2, m, lsum, acc, valid):
    width = kp.shape[0]
    cls = _lane_cls(width)
    s = _pick_cls(_dot_nt(q, kp.astype(BF16)), DIFF_ROW_CLS, cls)
    if valid is not None:
        s = jnp.where(valid, s, NEG)
    m_new = jnp.maximum(m, _allreduce_cls(_fold_chunks(s, jnp.maximum), jnp.maximum))
    alpha = jnp.exp(m - m_new)
    p = jnp.exp(s - _tile_lanes(m_new, width))
    if valid is not None:
        p = jnp.where(valid, p, 0.0)
    p_rows = _spread_cls(p, DIFF_ROW_CLS, cls)
    cls1 = _lane_cls(LANES)
    alpha_rows = jnp.concatenate(
        [jnp.max(jnp.where(cls1 == c, alpha, 0.0), axis=1, keepdims=True) for c in DIFF_ROW_CLS], axis=0)
    acc = alpha_rows * acc + _dot(p_rows.astype(BF16), v2)
    lsum = alpha_rows * lsum + _fold_chunks(p_rows, jnp.add)
    return m_new, lsum, acc


def _dup_rows(dst_ref, v):
    n = v.shape[0]
    dst_ref[pl.ds(0, n, stride=2), :] = v
    dst_ref[pl.ds(1, n, stride=2), :] = v
    return dst_ref[...].astype(BF16)


def _decode_kernel(pt_ref, qs_ref, qd_ref, ksn_ref, vsn_ref, kdn_ref, vdn_ref,
                   csk_ref, csv_ref, cdk_ref, cdv_ref,
                   lq1_ref, lk1_ref, lq2_ref, lk2_ref, g_ref,
                   so_ref, do_ref,
                   sacc_ref, scarry_ref, dacc_ref, dm_ref, dl_ref, v2_ref, v2n_ref):
    j = pl.program_id(1)
    qs = qs_ref[0]
    qd = qd_ref[0]

    @pl.when(j == 0)
    def _():
        lane = lax.broadcasted_iota(jnp.int32, (N_CLS, LANES), 1)
        qrow = lax.broadcasted_iota(jnp.int32, (N_CLS, LANES), 0)
        tok = lane // N_CLS
        carry, acc = _sb_page(qs, ksn_ref[0], vsn_ref[0],
                              jnp.zeros((N_CLS, LANES), F32), jnp.zeros((64, HEAD_DIM), F32), tok < qrow)
        scarry_ref[...] = carry
        sacc_ref[...] = acc
        m, lsum, dacc = _diff_page(qd, kdn_ref[0], _dup_rows(v2n_ref, vdn_ref[0]),
                                   jnp.full((N_CLS, LANES), NEG, F32), jnp.zeros((64, LANES), F32),
                                   jnp.zeros((64, DIFF_V_DIM), F32), tok <= qrow)
        dm_ref[...] = m
        dl_ref[...] = lsum
        dacc_ref[...] = dacc

    carry, acc = _sb_page(qs, csk_ref[0], csv_ref[0], scarry_ref[...], sacc_ref[...], None)
    scarry_ref[...] = carry
    sacc_ref[...] = acc
    m, lsum, dacc = _diff_page(qd, cdk_ref[0], _dup_rows(v2_ref, cdv_ref[0]),
                               dm_ref[...], dl_ref[...], dacc_ref[...], None)
    dm_ref[...] = m
    dl_ref[...] = lsum
    dacc_ref[...] = dacc

    @pl.when(j == pl.num_programs(1) - 1)
    def _():
        so_ref[0] = sacc_ref[...]
        on = dacc_ref[...] / jnp.sum(dl_ref[...], axis=1, keepdims=True)
        lam = _lambda_value(lq1_ref[...], lk1_ref[...], lq2_ref[...], lk2_ref[...])
        o = on[0:32] - lam * on[32:64]
        do_ref[0] = _sub_rms_norm(o, g_ref[...])


def _decode(page_table, qs, qd, ksn, vsn, kdn, vdn, csk, csv, cdk, cdv, lams, subln_g):
    n_seq, n_pages = page_table.shape
    rows_k = PAGE_SIZE * N_CLS
    rows_v = PAGE_SIZE * DIFF_HEADS
    seq_spec = lambda r, w: pl.BlockSpec((1, r, w), lambda s, j, pt: (s, 0, 0))
    page_spec = lambda r, w: pl.BlockSpec((1, r, w), lambda s, j, pt: (pt[s, n_pages - 1 - j], 0, 0))
    small = lambda w: pl.BlockSpec((1, w), lambda s, j, pt: (0, 0))
    grid_spec = pltpu.PrefetchScalarGridSpec(
        num_scalar_prefetch=1,
        grid=(n_seq, n_pages),
        in_specs=[seq_spec(64, HEAD_DIM), seq_spec(64, HEAD_DIM),
                  seq_spec(LANES, HEAD_DIM), seq_spec(LANES, HEAD_DIM), seq_spec(LANES, HEAD_DIM),
                  seq_spec(LANES // 2, DIFF_V_DIM),
                  page_spec(rows_k, HEAD_DIM), page_spec(rows_k, HEAD_DIM), page_spec(rows_k, HEAD_DIM),
                  page_spec(rows_v, DIFF_V_DIM),
                  small(HEAD_DIM), small(HEAD_DIM), small(HEAD_DIM), small(HEAD_DIM), small(DIFF_V_DIM)],
        out_specs=[seq_spec(64, HEAD_DIM), seq_spec(32, DIFF_V_DIM)],
        scratch_shapes=[pltpu.VMEM((64, HEAD_DIM), F32), pltpu.VMEM((N_CLS, LANES), F32),
                        pltpu.VMEM((64, DIFF_V_DIM), F32), pltpu.VMEM((N_CLS, LANES), F32),
                        pltpu.VMEM((64, LANES), F32),
                        pltpu.VMEM((rows_k, DIFF_V_DIM), F32), pltpu.VMEM((LANES, DIFF_V_DIM), F32)],
    )
    return pl.pallas_call(
        _decode_kernel,
        grid_spec=grid_spec,
        out_shape=[jax.ShapeDtypeStruct((n_seq, 64, HEAD_DIM), F32),
                   jax.ShapeDtypeStruct((n_seq, 32, DIFF_V_DIM), F32)],
        compiler_params=_params(("arbitrary", "arbitrary")),
        name="decode",
    )(page_table, qs, qd, ksn, vsn, kdn, vdn, csk, csv, cdk, cdv, *lams, subln_g)


def _layer_norm(y, g, b):
    mu = jnp.mean(y, axis=1, keepdims=True)
    d = y - mu
    var = jnp.mean(d * d, axis=1, keepdims=True)
    return d * lax.rsqrt(var + LN_EPS) * g + b


def _merge_kernel(x_ref, sb_ref, d_ref, gate_ref, w_ref, g_ref, b_ref, o_ref):
    o = _dot(sb_ref[0], w_ref[0:SB_WIDTH, :]) + _dot(d_ref[0], w_ref[SB_WIDTH:, :])
    y = DEEPNORM_ALPHA * x_ref[0] + gate_ref[0] * o
    o_ref[0] = _layer_norm(y, g_ref[...], b_ref[...])


def _merge(x, sb_o, d_o, mod, w_out, g, b, tm):
    grp, rows, _ = x.shape
    row_spec = lambda width: pl.BlockSpec((1, tm, width), lambda bb, i: (bb, i, 0))
    vec = pl.BlockSpec((1, D_MODEL), lambda bb, i: (0, 0))
    return pl.pallas_call(
        _merge_kernel,
        grid=(grp, rows // tm),
        in_specs=[row_spec(D_MODEL), row_spec(SB_WIDTH), row_spec(SB_WIDTH), _mod_spec(mod, tm, 2),
                  pl.BlockSpec((D_MODEL, D_MODEL), lambda bb, i: (0, 0)), vec, vec],
        out_specs=row_spec(D_MODEL),
        out_shape=jax.ShapeDtypeStruct(x.shape, F32),
        compiler_params=_params(("arbitrary", "arbitrary")),
        name="merge",
    )(x, sb_o, d_o, mod, w_out, g, b)


def _ffn_kernel(x_ref, shift_ref, scale_ref, gate_ref, wg_ref, wu_ref, wd_ref, g_ref, b_ref, o_ref, acc_ref):
    c = pl.program_id(2)
    x = x_ref[0]
    h = (x * (1.0 + scale_ref[0]) + shift_ref[0]).astype(BF16)
    gt = _dot(h, wg_ref[...])
    up = _dot(h, wu_ref[...])
    f = (gt * (1.0 / (1.0 + jnp.exp(-gt))) * up).astype(BF16)
    part = _dot(f, wd_ref[...])

    @pl.when(c == 0)
    def _():
        acc_ref[...] = part

    @pl.when(c != 0)
    def _():
        acc_ref[...] += part

    @pl.when(c == pl.num_programs(2) - 1)
    def _():
        y = DEEPNORM_ALPHA * x + gate_ref[0] * acc_ref[...]
        o_ref[0] = _layer_norm(y, g_ref[...], b_ref[...])


def _ffn(x, mod, w_gu, w_dn, g, b, tm, n_chunks):
    grp, rows, _ = x.shape
    fc = D_FF // n_chunks
    row_spec = pl.BlockSpec((1, tm, D_MODEL), lambda bb, i, c: (bb, i, 0))
    vec = pl.BlockSpec((1, D_MODEL), lambda bb, i, c: (0, 0))

    def mod_spec(col):
        if mod.shape[1] == 1:
            return pl.BlockSpec((1, 1, D_MODEL), lambda bb, i, c: (bb, 0, col))
        return pl.BlockSpec((1, tm, D_MODEL), lambda bb, i, c: (bb, i, col))

    return pl.pallas_call(
        _ffn_kernel,
        grid=(grp, rows // tm, n_chunks),
        in_specs=[row_spec, mod_spec(3), mod_spec(4), mod_spec(5),
                  pl.BlockSpec((D_MODEL, fc), lambda bb, i, c: (0, c)),
                  pl.BlockSpec((D_MODEL, fc), lambda bb, i, c: (0, n_chunks + c)),
                  pl.BlockSpec((fc, D_MODEL), lambda bb, i, c: (c, 0)), vec, vec],
        out_specs=row_spec,
        out_shape=jax.ShapeDtypeStruct(x.shape, F32),
        scratch_shapes=[pltpu.VMEM((tm, D_MODEL), F32)],
        compiler_params=_params(("arbitrary", "arbitrary", "arbitrary")),
        name="ffn",
    )(x, mod, mod, mod, w_gu, w_gu, w_dn, g, b)


def _rope_tables(pos):
    half = HEAD_DIM // 2
    inv = ROPE_THETA ** (-jnp.arange(half, dtype=F32) / half)
    ang = pos.astype(F32)[:, None] * inv[None, :]
    cos = jnp.cos(ang)
    sin = jnp.sin(ang)
    return jnp.concatenate([cos] * 4, axis=1), jnp.concatenate([-sin, sin, -sin, sin], axis=1)


def kernel(x_prompt, x_sample, cache_sb_k, cache_sb_v, cache_diff_k, cache_diff_v, page_table, c_prompt, c_sample, w_ada, b_ada, w_in, w_out, lambda_q1, lambda_k1, lambda_q2, lambda_k2, subln_g, ln1_g, ln1_b, w_gate_up, w_down, ln2_g, ln2_b):
    n_batch, seq, _ = x_prompt.shape
    n_dec, n_new, _ = x_sample.shape
    n_pool = cache_sb_k.shape[0]
    lyr = 0

    n_c = n_batch + n_dec
    pad = (-n_c) % 8
    c_all = jnp.concatenate([c_prompt, c_sample, jnp.zeros((pad, D_MODEL), F32)], axis=0)
    mod = _ada(c_all, w_ada[lyr], b_ada[lyr][None, :])
    mod_p = mod[:n_batch].reshape(n_batch, 1, 6 * D_MODEL)
    mod_s = jnp.repeat(mod[n_batch:n_c], n_new, axis=0).reshape(1, n_dec * n_new, 6 * D_MODEL)

    w_in_b = w_in[lyr].astype(BF16)
    w_out_b = w_out[lyr].astype(BF16)
    w_gu_b = w_gate_up[lyr].astype(BF16)
    w_dn_b = w_down[lyr].astype(BF16)
    lams = (lambda_q1, lambda_k1, lambda_q2, lambda_k2)
    g1, b1, g2, b2 = ln1_g[lyr][None, :], ln1_b[lyr][None, :], ln2_g[lyr][None, :], ln2_b[lyr][None, :]

    cos_p, sin_p = _rope_tables(jnp.arange(seq, dtype=jnp.int32))
    qkv_p, ksb_p, vsb_p, kd_p, vd_p = _inproj(x_prompt, mod_p, w_in_b, cos_p, sin_p, tm=512)
    sb_o_p = _sb_attn(qkv_p, tq=256)
    d_o_p = _diff_attn(qkv_p, lams, subln_g, tq=256)
    x1_p = _merge(x_prompt, sb_o_p, d_o_p, mod_p, w_out_b, g1, b1, tm=512)
    y_p = _ffn(x1_p, mod_p, w_gu_b, w_dn_b, g2, b2, tm=512, n_chunks=2)

    rows_s = n_dec * n_new
    cos_n, sin_n = _rope_tables(PAST_LEN + jnp.arange(n_new, dtype=jnp.int32))
    cos_s = jnp.tile(cos_n, (n_dec, 1))
    sin_s = jnp.tile(sin_n, (n_dec, 1))
    xs = x_sample.reshape(1, rows_s, D_MODEL)
    qkv_s, ksb_s, vsb_s, kd_s, vd_s = _inproj(xs, mod_s, w_in_b, cos_s, sin_s, tm=256)

    w = SB_WIDTH
    qs = qkv_s[0, :, 0:w].reshape(n_dec, n_new, SB_HEADS, HEAD_DIM).transpose(0, 2, 1, 3)
    qs = qs.reshape(n_dec, SB_HEADS * n_new, HEAD_DIM)
    qd = qkv_s[0, :, 3 * w:4 * w].reshape(n_dec, n_new, DIFF_HEADS, 2, HEAD_DIM).transpose(0, 3, 2, 1, 4)
    qd = qd.reshape(n_dec, 2 * DIFF_HEADS * n_new, HEAD_DIM)

    def new_rows(a, heads, dim):
        a = a.reshape(n_dec, n_new * heads, dim)
        return jnp.pad(a, ((0, 0), (0, n_new * heads), (0, 0)))

    so, do = _decode(
        page_table, qs, qd,
        new_rows(ksb_s, SB_HEADS, HEAD_DIM), new_rows(vsb_s, SB_HEADS, HEAD_DIM),
        new_rows(kd_s, 2 * DIFF_HEADS, HEAD_DIM), new_rows(vd_s, DIFF_HEADS, DIFF_V_DIM),
        cache_sb_k.reshape(n_pool, PAGE_SIZE * SB_HEADS, HEAD_DIM),
        cache_sb_v.reshape(n_pool, PAGE_SIZE * SB_HEADS, HEAD_DIM),
        cache_diff_k.reshape(n_pool, PAGE_SIZE * 2 * DIFF_HEADS, HEAD_DIM),
        cache_diff_v.reshape(n_pool, PAGE_SIZE * DIFF_HEADS, DIFF_V_DIM),
        lams, subln_g)
    sb_o_s = so.reshape(n_dec, SB_HEADS, n_new, HEAD_DIM).transpose(0, 2, 1, 3).reshape(1, rows_s, w)
    d_o_s = do.reshape(n_dec, DIFF_HEADS, n_new, DIFF_V_DIM).transpose(0, 2, 1, 3).reshape(1, rows_s, w)
    x1_s = _merge(xs, sb_o_s.astype(BF16), d_o_s.astype(BF16), mod_s, w_out_b, g1, b1, tm=256)
    y_s = _ffn(x1_s, mod_s, w_gu_b, w_dn_b, g2, b2, tm=256, n_chunks=2)

    return (y_p, y_s.reshape(n_dec, n_new, D_MODEL),
            ksb_p.reshape(n_batch, 1, seq, SB_HEADS, HEAD_DIM),
            vsb_p.reshape(n_batch, 1, seq, SB_HEADS, HEAD_DIM),
            kd_p.reshape(n_batch, 1, seq, 2 * DIFF_HEADS, HEAD_DIM),
            vd_p.reshape(n_batch, 1, seq, DIFF_HEADS, DIFF_V_DIM),
            ksb_s.reshape(n_dec, 1, n_new, SB_HEADS, HEAD_DIM),
            vsb_s.reshape(n_dec, 1, n_new, SB_HEADS, HEAD_DIM),
            kd_s.reshape(n_dec, 1, n_new, 2 * DIFF_HEADS, HEAD_DIM),
            vd_s.reshape(n_dec, 1, n_new, DIFF_HEADS, DIFF_V_DIM))
```

```python
import functools
import math

import numpy as np
import jax
import jax.numpy as jnp
from jax import lax
from jax.experimental import pallas as pl
from jax.experimental.pallas import tpu as pltpu

F32 = jnp.float32
BF16 = jnp.bfloat16

D_MODEL = 1024
HEAD_DIM = 64
SB_HEADS = 8
DIFF_HEADS = 4
SB_WIDTH = SB_HEADS * HEAD_DIM
DIFF_V_DIM = 2 * HEAD_DIM
IN_WIDTH = 6 * SB_WIDTH
D_FF = 2816
PAGE_SIZE = 128
PAST_LEN = 8192
ROPE_THETA = 10000.0
LN_EPS = 1e-5
DEPTH = 1
DEEPNORM_ALPHA = (2 * DEPTH) ** 0.25
LAM_INIT = 0.8 - 0.6 * math.exp(-0.3 * 0)
QK_SCALE = HEAD_DIM ** -0.5
NEG = -1e30

LANES = 128
VMEM_LIMIT = 56 * 1024 * 1024

COL_SBQ, COL_SBK, COL_SBV, COL_DQ, COL_DK, COL_DV = 0, 4, 8, 12, 16, 20

NT_DIMS = (((1,), (1,)), ((), ()))


def _dot(a, b):
    return jnp.dot(a, b, preferred_element_type=F32)


def _dot_nt(a, b):
    return lax.dot_general(a, b, NT_DIMS, preferred_element_type=F32)


def _params(sem):
    return pltpu.CompilerParams(dimension_semantics=sem, vmem_limit_bytes=VMEM_LIMIT)


def _ada_kernel(c_ref, w_ref, b_ref, o_ref):
    c = c_ref[...]
    a = c * (1.0 / (1.0 + jnp.exp(-c)))
    o_ref[...] = jnp.dot(a, w_ref[...], preferred_element_type=F32,
                         precision=lax.Precision.HIGHEST) + b_ref[...]


def _ada(c_all, w, b):
    rows = c_all.shape[0]
    n = w.shape[1]
    tn = 1024
    return pl.pallas_call(
        _ada_kernel,
        grid=(n // tn,),
        in_specs=[pl.BlockSpec((rows, D_MODEL), lambda k: (0, 0)),
                  pl.BlockSpec((D_MODEL, tn), lambda k: (0, k)),
                  pl.BlockSpec((1, tn), lambda k: (0, k))],
        out_specs=pl.BlockSpec((rows, tn), lambda k: (0, k)),
        out_shape=jax.ShapeDtypeStruct((rows, n), F32),
        compiler_params=_params(("arbitrary",)),
        name="ada",
    )(c_all, w, b)


def _inproj_kernel(x_ref, shift_ref, scale_ref, w_ref, cos_ref, sin_ref,
                   qkv_ref, ksb_ref, vsb_ref, kd_ref, vd_ref):
    x = x_ref[0]
    h = x * (1.0 + scale_ref[0]) + shift_ref[0]
    y = _dot(h.astype(BF16), w_ref[...])
    tm = y.shape[0]
    cos = cos_ref[...]
    sin = sin_ref[...]
    lane = lax.broadcasted_iota(jnp.int32, (tm, LANES), 1)
    first_half = (lane & (HEAD_DIM - 1)) < (HEAD_DIM // 2)

    def rope(t):
        rot = jnp.where(first_half, pltpu.roll(t, LANES - HEAD_DIM // 2, 1),
                        pltpu.roll(t, HEAD_DIM // 2, 1))
        return t * cos + rot * sin

    w = SB_WIDTH
    qkv_ref[0, :, 0:w] = (y[:, 0:w] * QK_SCALE).astype(BF16)
    ksb = y[:, w:2 * w]
    vsb = y[:, 2 * w:3 * w]
    ksb_ref[0] = ksb
    vsb_ref[0] = vsb
    qkv_ref[0, :, w:2 * w] = ksb.astype(BF16)
    qkv_ref[0, :, 2 * w:3 * w] = vsb.astype(BF16)
    for g in range(w // LANES):
        lo = 3 * w + g * LANES
        qkv_ref[0, :, lo:lo + LANES] = (rope(y[:, lo:lo + LANES]) * QK_SCALE).astype(BF16)
        lo = 4 * w + g * LANES
        kd = rope(y[:, lo:lo + LANES])
        kd_ref[0, :, g * LANES:(g + 1) * LANES] = kd
        qkv_ref[0, :, lo:lo + LANES] = kd.astype(BF16)
    vd = y[:, 5 * w:6 * w]
    vd_ref[0] = vd
    qkv_ref[0, :, 5 * w:6 * w] = vd.astype(BF16)


def _mod_spec(mod, tm, col):
    if mod.shape[1] == 1:
        return pl.BlockSpec((1, 1, D_MODEL), lambda b, i: (b, 0, col))
    return pl.BlockSpec((1, tm, D_MODEL), lambda b, i: (b, i, col))


def _inproj(x, mod, w_in, cos, sin, tm):
    g, rows, _ = x.shape
    row_spec = lambda width: pl.BlockSpec((1, tm, width), lambda b, i: (b, i, 0))
    return pl.pallas_call(
        _inproj_kernel,
        grid=(g, rows // tm),
        in_specs=[row_spec(D_MODEL), _mod_spec(mod, tm, 0), _mod_spec(mod, tm, 1),
                  pl.BlockSpec((D_MODEL, IN_WIDTH), lambda b, i: (0, 0)),
                  pl.BlockSpec((tm, LANES), lambda b, i: (i, 0)),
                  pl.BlockSpec((tm, LANES), lambda b, i: (i, 0))],
        out_specs=[row_spec(IN_WIDTH)] + [row_spec(SB_WIDTH)] * 4,
        out_shape=[jax.ShapeDtypeStruct((g, rows, IN_WIDTH), BF16)]
                  + [jax.ShapeDtypeStruct((g, rows, SB_WIDTH), F32)] * 4,
        compiler_params=_params(("arbitrary", "arbitrary")),
        name="inproj",
    )(x, mod, mod, w_in, cos, sin)


def _log_gates(z):
    sp = jnp.log(1.0 + jnp.exp(-jnp.abs(z)))
    l = jnp.minimum(-z, 0.0) - sp
    return l, l + z


def _triangle(n):
    j = np.arange(n)
    tri = (j[:, None] > j[None, :]).astype(np.float32)
    return jnp.asarray(np.concatenate([tri, tri], axis=0), BF16)


def _tail_sums(l, u2):
    l_hi = l.astype(BF16)
    l_lo = (l - l_hi.astype(F32)).astype(BF16)
    return _dot(jnp.concatenate([l_hi, l_lo], axis=1), u2)


def _lambda_value(lq1, lk1, lq2, lk2):
    return (jnp.exp(jnp.sum(lq1 * lk1, axis=1, keepdims=True))
            - jnp.exp(jnp.sum(lq2 * lk2, axis=1, keepdims=True)) + LAM_INIT)


def _sub_rms_norm(o, g):
    ms = jnp.mean(o * o, axis=1, keepdims=True)
    return o * lax.rsqrt(ms + LN_EPS) * g * (1.0 - LAM_INIT)


SB_HEADS_PER_STEP = 8
DIFF_HEADS_PER_STEP = 4


def _tri_tables(nq, descending):
    qs, ks = [], []
    for qi in range(nq):
        order = range(qi, -1, -1) if descending else range(qi + 1)
        for kj in order:
            qs.append(qi)
            ks.append(kj)
    return jnp.asarray(np.array(qs, np.int32)), jnp.asarray(np.array(ks, np.int32))


def _sb_attn_kernel(qi_ref, kj_ref, q_ref, k_ref, v_ref, u_ref, o_ref, acc_ref, carry_ref):
    t = pl.program_id(2)
    qi = qi_ref[t]
    kj = kj_ref[t]
    tq = q_ref.shape[1]

    @pl.when(kj == qi)
    def _():
        acc_ref[...] = jnp.zeros_like(acc_ref)
        carry_ref[...] = jnp.zeros_like(carry_ref)

    def step(diag):
        q = q_ref[0]
        k = k_ref[0]
        v = v_ref[0]
        if diag:
            row = lax.broadcasted_iota(jnp.int32, (tq, tq), 0)
            col = lax.broadcasted_iota(jnp.int32, (tq, tq), 1)
            mask = col < row
        heads = range(SB_HEADS_PER_STEP)
        sl = [slice(h * HEAD_DIM, (h + 1) * HEAD_DIM) for h in heads]
        gates = [_log_gates(_dot_nt(q[:, sl[h]], k[:, sl[h]])) for h in heads]
        ls = [g[1] for g in gates]
        l = [jnp.where(mask, g[0], 0.0) if diag else g[0] for g in gates]
        tails = _tail_sums(jnp.concatenate(l, axis=0), u_ref[...])
        for h in heads:
            tail = tails[h * tq:(h + 1) * tq]
            carry = carry_ref[h]
            w = jnp.exp(ls[h] + tail + carry)
            if diag:
                w = jnp.where(mask, w, 0.0)
            acc_ref[h] += _dot(w.astype(BF16), v[:, sl[h]])
            carry_ref[h] = carry + tail[:, 0:1] + l[h][:, 0:1]

    pl.when(kj == qi)(functools.partial(step, True))
    pl.when(kj != qi)(functools.partial(step, False))

    @pl.when(kj == 0)
    def _():
        o_ref[0] = jnp.concatenate([acc_ref[h] for h in range(SB_HEADS_PER_STEP)], axis=1).astype(o_ref.dtype)


def _sb_attn(qkv, tq):
    b, s, _ = qkv.shape
    qi_tab, kj_tab = _tri_tables(s // tq, descending=True)
    nh = SB_HEADS_PER_STEP
    width = nh * HEAD_DIM
    groups = SB_WIDTH // width
    col = lambda first: first * LANES // width
    grid_spec = pltpu.PrefetchScalarGridSpec(
        num_scalar_prefetch=2,
        grid=(b, groups, int(qi_tab.shape[0])),
        in_specs=[pl.BlockSpec((1, tq, width), lambda bb, g, t, qt, kt: (bb, qt[t], col(COL_SBQ) + g)),
                  pl.BlockSpec((1, tq, width), lambda bb, g, t, qt, kt: (bb, kt[t], col(COL_SBK) + g)),
                  pl.BlockSpec((1, tq, width), lambda bb, g, t, qt, kt: (bb, kt[t], col(COL_SBV) + g)),
                  pl.BlockSpec((2 * tq, tq), lambda bb, g, t, qt, kt: (0, 0))],
        out_specs=pl.BlockSpec((1, tq, width), lambda bb, g, t, qt, kt: (bb, qt[t], g)),
        scratch_shapes=[pltpu.VMEM((nh, tq, HEAD_DIM), F32), pltpu.VMEM((nh, tq, 1), F32)],
    )
    return pl.pallas_call(
        _sb_attn_kernel,
        grid_spec=grid_spec,
        out_shape=jax.ShapeDtypeStruct((b, s, SB_WIDTH), BF16),
        compiler_params=_params(("arbitrary", "arbitrary", "arbitrary")),
        name="sb_attn",
    )(qi_tab, kj_tab, qkv, qkv, qkv, _triangle(tq))


def _diff_attn_kernel(qi_ref, kj_ref, q_ref, k_ref, v_ref, lq1_ref, lk1_ref, lq2_ref, lk2_ref, g_ref,
                      o_ref, acc_ref, m_ref, l_ref):
    t = pl.program_id(2)
    qi = qi_ref[t]
    kj = kj_ref[t]
    tq = q_ref.shape[1]

    @pl.when(kj == 0)
    def _():
        acc_ref[...] = jnp.zeros_like(acc_ref)
        l_ref[...] = jnp.zeros_like(l_ref)
        m_ref[...] = jnp.full_like(m_ref, NEG)

    def step(diag):
        q = q_ref[0]
        k = k_ref[0]
        v = v_ref[0]
        if diag:
            row = lax.broadcasted_iota(jnp.int32, (tq, tq), 0)
            col = lax.broadcasted_iota(jnp.int32, (tq, tq), 1)
            mask = col <= row
        maps = range(2 * DIFF_HEADS_PER_STEP)
        sl = [slice(i * HEAD_DIM, (i + 1) * HEAD_DIM) for i in maps]
        s = [_dot_nt(q[:, sl[i]], k[:, sl[i]]) for i in maps]
        if diag:
            s = [jnp.where(mask, si, NEG) for si in s]
        m_prev = [m_ref[i] for i in maps]
        m_new = [jnp.maximum(m_prev[i], jnp.max(s[i], axis=1, keepdims=True)) for i in maps]
        alpha = [jnp.exp(m_prev[i] - m_new[i]) for i in maps]
        p = [jnp.exp(s[i] - m_new[i]) for i in maps]
        for i in maps:
            vh = v[:, (i // 2) * DIFF_V_DIM:(i // 2 + 1) * DIFF_V_DIM]
            l_ref[i] = alpha[i] * l_ref[i] + jnp.sum(p[i], axis=1, keepdims=True)
            acc_ref[i] = alpha[i] * acc_ref[i] + _dot(p[i].astype(BF16), vh)
            m_ref[i] = m_new[i]

    pl.when(kj == qi)(functools.partial(step, True))
    pl.when(kj != qi)(functools.partial(step, False))

    @pl.when(kj == qi)
    def _():
        lam = _lambda_value(lq1_ref[...], lk1_ref[...], lq2_ref[...], lk2_ref[...])
        for hv in range(DIFF_HEADS_PER_STEP):
            o = acc_ref[2 * hv] / l_ref[2 * hv] - lam * (acc_ref[2 * hv + 1] / l_ref[2 * hv + 1])
            o_ref[0, :, hv * DIFF_V_DIM:(hv + 1) * DIFF_V_DIM] = _sub_rms_norm(o, g_ref[...]).astype(o_ref.dtype)


def _diff_attn(qkv, lams, subln_g, tq):
    b, s, _ = qkv.shape
    qi_tab, kj_tab = _tri_tables(s // tq, descending=False)
    nhv = DIFF_HEADS_PER_STEP
    width = nhv * DIFF_V_DIM
    groups = DIFF_HEADS // nhv
    col = lambda first: first * LANES // width
    small = lambda w: pl.BlockSpec((1, w), lambda bb, g, t, qt, kt: (0, 0))
    grid_spec = pltpu.PrefetchScalarGridSpec(
        num_scalar_prefetch=2,
        grid=(b, groups, int(qi_tab.shape[0])),
        in_specs=[pl.BlockSpec((1, tq, width), lambda bb, g, t, qt, kt: (bb, qt[t], col(COL_DQ) + g)),
                  pl.BlockSpec((1, tq, width), lambda bb, g, t, qt, kt: (bb, kt[t], col(COL_DK) + g)),
                  pl.BlockSpec((1, tq, width), lambda bb, g, t, qt, kt: (bb, kt[t], col(COL_DV) + g)),
                  small(HEAD_DIM), small(HEAD_DIM), small(HEAD_DIM), small(HEAD_DIM), small(DIFF_V_DIM)],
        out_specs=pl.BlockSpec((1, tq, width), lambda bb, g, t, qt, kt: (bb, qt[t], g)),
        scratch_shapes=[pltpu.VMEM((2 * nhv, tq, DIFF_V_DIM), F32), pltpu.VMEM((2 * nhv, tq, 1), F32),
                        pltpu.VMEM((2 * nhv, tq, 1), F32)],
    )
    return pl.pallas_call(
        _diff_attn_kernel,
        grid_spec=grid_spec,
        out_shape=jax.ShapeDtypeStruct((b, s, SB_WIDTH), BF16),
        compiler_params=_params(("arbitrary", "arbitrary", "arbitrary")),
        name="diff_attn",
    )(qi_tab, kj_tab, qkv, qkv, qkv, *lams, subln_g)


DEC_PAGES_PER_STEP = 8
DEC_ROWS = 64


def _decode_kernel(pt_ref, qs_ref, qd_ref, ksn_ref, vsn_ref, kdn_ref, vdn_ref, *rest):
    npg = DEC_PAGES_PER_STEP
    pages = rest[:4 * npg]
    (u_ref, lq1_ref, lk1_ref, lq2_ref, lk2_ref, g_ref, so_ref, do_ref,
     sacc_ref, scarry_ref, dacc_ref, dm_ref, dl_ref, pad_ref) = rest[4 * npg:]
    j = pl.program_id(1)
    qs = qs_ref[0]
    qd = qd_ref[0]
    n_new = ksn_ref.shape[1]

    def sb_tile(z, valid, pv):
        l, ls = _log_gates(z)
        if valid is not None:
            l = jnp.where(valid, l, 0.0)
        n = z.shape[1] // PAGE_SIZE
        chunk = lambda a, p: a[:, p * PAGE_SIZE:(p + 1) * PAGE_SIZE]
        tails = _tail_sums(jnp.concatenate([chunk(l, p) for p in range(n)], axis=0), u_ref[...])
        carry = scarry_ref[...]
        ws = []
        for p in range(n):
            tail = tails[p * DEC_ROWS:(p + 1) * DEC_ROWS]
            ws.append(jnp.exp(chunk(ls, p) + tail + carry))
            carry = carry + tail[:, 0:1] + chunk(l, p)[:, 0:1]
        w = ws[0] if n == 1 else jnp.concatenate(ws, axis=1)
        if valid is not None:
            w = jnp.where(valid, w, 0.0)
        sacc_ref[...] += pv(w.astype(BF16))
        scarry_ref[...] = carry

    def diff_tile(s, valid, v_of):
        if valid is not None:
            s = jnp.where(valid, s, NEG)
        m_prev = dm_ref[...]
        m_new = jnp.maximum(m_prev, jnp.max(s, axis=1, keepdims=True))
        alpha = jnp.exp(m_prev - m_new)
        p = jnp.exp(s - m_new)
        if valid is not None:
            p = jnp.where(valid, p, 0.0)
        dl_ref[...] = alpha * dl_ref[...] + jnp.sum(p, axis=1, keepdims=True)
        dm_ref[...] = m_new
        pb = p.astype(BF16)
        for hv in range(DIFF_HEADS):
            rows = slice(16 * hv, 16 * hv + 16)
            dacc_ref[rows, :] = alpha[rows] * dacc_ref[rows, :] + _dot(pb[rows], v_of(hv))

    @pl.when(j == 0)
    def _():
        sacc_ref[...] = jnp.zeros_like(sacc_ref)
        scarry_ref[...] = jnp.zeros_like(scarry_ref)
        dacc_ref[...] = jnp.zeros_like(dacc_ref)
        dl_ref[...] = jnp.zeros_like(dl_ref)
        dm_ref[...] = jnp.full_like(dm_ref, NEG)
        pad_ref[...] = jnp.zeros_like(pad_ref)
        pad_ref[0, 0:n_new, :] = ksn_ref[0]
        pad_ref[1, 0:n_new, :] = vsn_ref[0]
        pad_ref[2, 0:n_new, :] = kdn_ref[0]
        pad_ref[3, 0:n_new, :] = vdn_ref[0]
        tok = lax.broadcasted_iota(jnp.int32, (DEC_ROWS, PAGE_SIZE), 1)
        qpos = lax.broadcasted_iota(jnp.int32, (DEC_ROWS, PAGE_SIZE), 0) & (n_new - 1)
        vn = pad_ref[1].astype(BF16)
        sb_tile(_dot_nt(qs, pad_ref[0].astype(BF16)), tok < qpos, lambda w: _dot(w, vn))
        vdn = pad_ref[3].astype(BF16)
        diff_tile(_dot_nt(qd, pad_ref[2].astype(BF16)), tok <= qpos,
                  lambda hv: vdn[:, hv * DIFF_V_DIM:(hv + 1) * DIFF_V_DIM])

    side_by_side = lambda which: jnp.concatenate(
        [pages[4 * p + which][...] for p in range(npg)], axis=1).astype(BF16)
    sb_tile(_dot(qs, side_by_side(0)), None, lambda w: _dot_nt(w, side_by_side(1)))
    diff_tile(_dot(qd, side_by_side(2)), None,
              lambda hv: jnp.concatenate(
                  [pages[4 * p + 3][pl.ds(hv, PAGE_SIZE, stride=DIFF_HEADS), :] for p in range(npg)],
                  axis=0).astype(BF16))

    @pl.when(j == pl.num_programs(1) - 1)
    def _():
        for h in range(SB_HEADS):
            so_ref[0, 8 * h:8 * h + 8, :] = sacc_ref[8 * h:8 * h + 8, h * HEAD_DIM:(h + 1) * HEAD_DIM]
        on = dacc_ref[...] / dl_ref[...]
        lam = _lambda_value(lq1_ref[...], lk1_ref[...], lq2_ref[...], lk2_ref[...])
        for hv in range(DIFF_HEADS):
            o = on[16 * hv:16 * hv + 8] - lam * on[16 * hv + 8:16 * hv + 16]
            do_ref[0, 8 * hv:8 * hv + 8, :] = _sub_rms_norm(o, g_ref[...])


def _decode(page_table, qs, qd, ksn, vsn, kdn, vdn, csk, csv, cdk, cdv, lams, subln_g):
    n_seq, n_pages = page_table.shape
    npg = DEC_PAGES_PER_STEP
    assert n_pages % npg == 0, (n_pages, npg)
    n_new = ksn.shape[1]
    rows = csk.shape[1]
    seq_spec = lambda r, w: pl.BlockSpec((1, r, w), lambda s, j, pt: (s, 0, 0))
    small = lambda w: pl.BlockSpec((1, w), lambda s, j, pt: (0, 0))

    def page_spec(p):
        return pl.BlockSpec((None, rows, PAGE_SIZE), lambda s, j, pt: (pt[s, n_pages - 1 - (j * npg + p)], 0, 0))

    page_specs, page_args = [], []
    for p in range(npg):
        page_specs += [page_spec(p)] * 4
        page_args += [csk, csv, cdk, cdv]
    grid_spec = pltpu.PrefetchScalarGridSpec(
        num_scalar_prefetch=1,
        grid=(n_seq, n_pages // npg),
        in_specs=[seq_spec(DEC_ROWS, rows), seq_spec(DEC_ROWS, rows)] + [seq_spec(n_new, rows)] * 4
                 + page_specs
                 + [pl.BlockSpec((2 * PAGE_SIZE, PAGE_SIZE), lambda s, j, pt: (0, 0)),
                    small(HEAD_DIM), small(HEAD_DIM), small(HEAD_DIM), small(HEAD_DIM), small(DIFF_V_DIM)],
        out_specs=[seq_spec(DEC_ROWS, HEAD_DIM), seq_spec(DEC_ROWS // 2, DIFF_V_DIM)],
        scratch_shapes=[pltpu.VMEM((DEC_ROWS, rows), F32), pltpu.VMEM((DEC_ROWS, 1), F32),
                        pltpu.VMEM((DEC_ROWS, DIFF_V_DIM), F32), pltpu.VMEM((DEC_ROWS, 1), F32),
                        pltpu.VMEM((DEC_ROWS, 1), F32), pltpu.VMEM((4, PAGE_SIZE, rows), F32)],
    )
    return pl.pallas_call(
        _decode_kernel,
        grid_spec=grid_spec,
        out_shape=[jax.ShapeDtypeStruct((n_seq, DEC_ROWS, HEAD_DIM), F32),
                   jax.ShapeDtypeStruct((n_seq, DEC_ROWS // 2, DIFF_V_DIM), F32)],
        compiler_params=_params(("arbitrary", "arbitrary")),
        name="decode",
    )(page_table, qs, qd, ksn, vsn, kdn, vdn, *page_args, _triangle(PAGE_SIZE), *lams, subln_g)


def _layer_norm(y, g, b):
    mu = jnp.mean(y, axis=1, keepdims=True)
    d = y - mu
    var = jnp.mean(d * d, axis=1, keepdims=True)
    return d * lax.rsqrt(var + LN_EPS) * g + b


def _merge_kernel(x_ref, sb_ref, d_ref, gate_ref, w_ref, g_ref, b_ref, o_ref):
    o = _dot(sb_ref[0], w_ref[0:SB_WIDTH, :]) + _dot(d_ref[0], w_ref[SB_WIDTH:, :])
    y = DEEPNORM_ALPHA * x_ref[0] + gate_ref[0] * o
    o_ref[0] = _layer_norm(y, g_ref[...], b_ref[...])


def _merge(x, sb_o, d_o, mod, w_out, g, b, tm):
    grp, rows, _ = x.shape
    row_spec = lambda width: pl.BlockSpec((1, tm, width), lambda bb, i: (bb, i, 0))
    vec = pl.BlockSpec((1, D_MODEL), lambda bb, i: (0, 0))
    return pl.pallas_call(
        _merge_kernel,
        grid=(grp, rows // tm),
        in_specs=[row_spec(D_MODEL), row_spec(SB_WIDTH), row_spec(SB_WIDTH), _mod_spec(mod, tm, 2),
                  pl.BlockSpec((D_MODEL, D_MODEL), lambda bb, i: (0, 0)), vec, vec],
        out_specs=row_spec(D_MODEL),
        out_shape=jax.ShapeDtypeStruct(x.shape, F32),
        compiler_params=_params(("arbitrary", "arbitrary")),
        name="merge",
    )(x, sb_o, d_o, mod, w_out, g, b)


def _ffn_kernel(x_ref, shift_ref, scale_ref, gate_ref, wg_ref, wu_ref, wd_ref, g_ref, b_ref, o_ref, acc_ref):
    c = pl.program_id(2)
    x = x_ref[0]
    h = (x * (1.0 + scale_ref[0]) + shift_ref[0]).astype(BF16)
    gt = _dot(h, wg_ref[...])
    up = _dot(h, wu_ref[...])
    f = (gt * (1.0 / (1.0 + jnp.exp(-gt))) * up).astype(BF16)
    part = _dot(f, wd_ref[...])

    @pl.when(c == 0)
    def _():
        acc_ref[...] = part

    @pl.when(c != 0)
    def _():
        acc_ref[...] += part

    @pl.when(c == pl.num_programs(2) - 1)
    def _():
        y = DEEPNORM_ALPHA * x + gate_ref[0] * acc_ref[...]
        o_ref[0] = _layer_norm(y, g_ref[...], b_ref[...])


def _ffn(x, mod, w_gu, w_dn, g, b, tm, n_chunks):
    grp, rows, _ = x.shape
    fc = D_FF // n_chunks
    row_spec = pl.BlockSpec((1, tm, D_MODEL), lambda bb, i, c: (bb, i, 0))
    vec = pl.BlockSpec((1, D_MODEL), lambda bb, i, c: (0, 0))

    def mod_spec(col):
        if mod.shape[1] == 1:
            return pl.BlockSpec((1, 1, D_MODEL), lambda bb, i, c: (bb, 0, col))
        return pl.BlockSpec((1, tm, D_MODEL), lambda bb, i, c: (bb, i, col))

    return pl.pallas_call(
        _ffn_kernel,
        grid=(grp, rows // tm, n_chunks),
        in_specs=[row_spec, mod_spec(3), mod_spec(4), mod_spec(5),
                  pl.BlockSpec((D_MODEL, fc), lambda bb, i, c: (0, c)),
                  pl.BlockSpec((D_MODEL, fc), lambda bb, i, c: (0, n_chunks + c)),
                  pl.BlockSpec((fc, D_MODEL), lambda bb, i, c: (c, 0)), vec, vec],
        out_specs=row_spec,
        out_shape=jax.ShapeDtypeStruct(x.shape, F32),
        scratch_shapes=[pltpu.VMEM((tm, D_MODEL), F32)],
        compiler_params=_params(("arbitrary", "arbitrary", "arbitrary")),
        name="ffn",
    )(x, mod, mod, mod, w_gu, w_gu, w_dn, g, b)


def _rope_tables(pos):
    half = HEAD_DIM // 2
    inv = ROPE_THETA ** (-jnp.arange(half, dtype=F32) / half)
    ang = pos.astype(F32)[:, None] * inv[None, :]
    cos = jnp.cos(ang)
    sin = jnp.sin(ang)
    return jnp.concatenate([cos] * 4, axis=1), jnp.concatenate([-sin, sin, -sin, sin], axis=1)


def kernel(x_prompt, x_sample, cache_sb_k, cache_sb_v, cache_diff_k, cache_diff_v, page_table, c_prompt, c_sample, w_ada, b_ada, w_in, w_out, lambda_q1, lambda_k1, lambda_q2, lambda_k2, subln_g, ln1_g, ln1_b, w_gate_up, w_down, ln2_g, ln2_b):
    n_batch, seq, _ = x_prompt.shape
    n_dec, n_new, _ = x_sample.shape
    n_pool = cache_sb_k.shape[0]
    lyr = 0

    n_c = n_batch + n_dec
    pad = (-n_c) % 8
    c_all = jnp.concatenate([c_prompt, c_sample, jnp.zeros((pad, D_MODEL), F32)], axis=0)
    mod = _ada(c_all, w_ada[lyr], b_ada[lyr][None, :])
    mod_p = mod[:n_batch].reshape(n_batch, 1, 6 * D_MODEL)
    mod_s = jnp.repeat(mod[n_batch:n_c], n_new, axis=0).reshape(1, n_dec * n_new, 6 * D_MODEL)

    w_in_b = w_in[lyr].astype(BF16)
    w_out_b = w_out[lyr].astype(BF16)
    w_gu_b = w_gate_up[lyr].astype(BF16)
    w_dn_b = w_down[lyr].astype(BF16)
    lams = (lambda_q1, lambda_k1, lambda_q2, lambda_k2)
    g1, b1, g2, b2 = ln1_g[lyr][None, :], ln1_b[lyr][None, :], ln2_g[lyr][None, :], ln2_b[lyr][None, :]

    cos_p, sin_p = _rope_tables(jnp.arange(seq, dtype=jnp.int32))
    qkv_p, ksb_p, vsb_p, kd_p, vd_p = _inproj(x_prompt, mod_p, w_in_b, cos_p, sin_p, tm=512)
    sb_o_p = _sb_attn(qkv_p, tq=256)
    d_o_p = _diff_attn(qkv_p, lams, subln_g, tq=256)
    x1_p = _merge(x_prompt, sb_o_p, d_o_p, mod_p, w_out_b, g1, b1, tm=512)
    y_p = _ffn(x1_p, mod_p, w_gu_b, w_dn_b, g2, b2, tm=512, n_chunks=2)

    rows_s = n_dec * n_new
    cos_n, sin_n = _rope_tables(PAST_LEN + jnp.arange(n_new, dtype=jnp.int32))
    cos_s = jnp.tile(cos_n, (n_dec, 1))
    sin_s = jnp.tile(sin_n, (n_dec, 1))
    xs = x_sample.reshape(1, rows_s, D_MODEL)
    qkv_s, ksb_s, vsb_s, kd_s, vd_s = _inproj(xs, mod_s, w_in_b, cos_s, sin_s, tm=256)

    w = SB_WIDTH

    def block_diag_rows(q):
        q = q.reshape(n_dec, n_new, SB_HEADS, HEAD_DIM).transpose(0, 2, 1, 3)
        same = jnp.eye(SB_HEADS, dtype=bool)[None, :, None, :, None]
        return jnp.where(same, q[:, :, :, None, :], 0).reshape(n_dec, SB_HEADS * n_new, w)

    def token_minor(cache):
        return cache.transpose(0, 1, 3, 4, 2).reshape(n_pool, w, PAGE_SIZE)

    so, do = _decode(
        page_table, block_diag_rows(qkv_s[0, :, 0:w]), block_diag_rows(qkv_s[0, :, 3 * w:4 * w]),
        ksb_s.reshape(n_dec, n_new, w), vsb_s.reshape(n_dec, n_new, w),
        kd_s.reshape(n_dec, n_new, w), vd_s.reshape(n_dec, n_new, w),
        token_minor(cache_sb_k), token_minor(cache_sb_v), token_minor(cache_diff_k),
        cache_diff_v.reshape(n_pool, PAGE_SIZE * DIFF_HEADS, DIFF_V_DIM),
        lams, subln_g)
    sb_o_s = so.reshape(n_dec, SB_HEADS, n_new, HEAD_DIM).transpose(0, 2, 1, 3).reshape(1, rows_s, w)
    d_o_s = do.reshape(n_dec, DIFF_HEADS, n_new, DIFF_V_DIM).transpose(0, 2, 1, 3).reshape(1, rows_s, w)
    x1_s = _merge(xs, sb_o_s.astype(BF16), d_o_s.astype(BF16), mod_s, w_out_b, g1, b1, tm=256)
    y_s = _ffn(x1_s, mod_s, w_gu_b, w_dn_b, g2, b2, tm=256, n_chunks=2)

    return (y_p, y_s.reshape(n_dec, n_new, D_MODEL),
            ksb_p.reshape(n_batch, 1, seq, SB_HEADS, HEAD_DIM),
            vsb_p.reshape(n_batch, 1, seq, SB_HEADS, HEAD_DIM),
            kd_p.reshape(n_batch, 1, seq, 2 * DIFF_HEADS, HEAD_DIM),
            vd_p.reshape(n_batch, 1, seq, DIFF_HEADS, DIFF_V_DIM),
            ksb_s.reshape(n_dec, 1, n_new, SB_HEADS, HEAD_DIM),
            vsb_s.reshape(n_dec, 1, n_new, SB_HEADS, HEAD_DIM),
            kd_s.reshape(n_dec, 1, n_new, 2 * DIFF_HEADS, HEAD_DIM),
            vd_s.reshape(n_dec, 1, n_new, DIFF_HEADS, DIFF_V_DIM))
```

```python
import functools
import math

import numpy as np
import jax
import jax.numpy as jnp
from jax import lax
from jax.experimental import pallas as pl
from jax.experimental.pallas import tpu as pltpu

F32 = jnp.float32
BF16 = jnp.bfloat16

D_MODEL = 1024
HEAD_DIM = 64
SB_HEADS = 8
DIFF_HEADS = 4
SB_WIDTH = SB_HEADS * HEAD_DIM
DIFF_V_DIM = 2 * HEAD_DIM
IN_WIDTH = 6 * SB_WIDTH
D_FF = 2816
PAGE_SIZE = 128
PAST_LEN = 8192
ROPE_THETA = 10000.0
LN_EPS = 1e-5
DEPTH = 1
DEEPNORM_ALPHA = (2 * DEPTH) ** 0.25
LAM_INIT = 0.8 - 0.6 * math.exp(-0.3 * 0)
QK_SCALE = HEAD_DIM ** -0.5
NEG = -1e30

LANES = 128
VMEM_LIMIT = 56 * 1024 * 1024

COL_SBQ, COL_SBK, COL_SBV, COL_DQ, COL_DK, COL_DV = 0, 4, 8, 12, 16, 20

NT_DIMS = (((1,), (1,)), ((), ()))


def _dot(a, b):
    return jnp.dot(a, b, preferred_element_type=F32)


def _dot_nt(a, b):
    return lax.dot_general(a, b, NT_DIMS, preferred_element_type=F32)


def _params(sem):
    return pltpu.CompilerParams(dimension_semantics=sem, vmem_limit_bytes=VMEM_LIMIT)


def _ada_kernel(c_ref, w_ref, b_ref, o_ref):
    c = c_ref[...]
    a = c * (1.0 / (1.0 + jnp.exp(-c)))
    o_ref[...] = jnp.dot(a, w_ref[...], preferred_element_type=F32,
                         precision=lax.Precision.HIGHEST) + b_ref[...]


def _ada(c_all, w, b):
    rows = c_all.shape[0]
    n = w.shape[1]
    tn = 1024
    return pl.pallas_call(
        _ada_kernel,
        grid=(n // tn,),
        in_specs=[pl.BlockSpec((rows, D_MODEL), lambda k: (0, 0)),
                  pl.BlockSpec((D_MODEL, tn), lambda k: (0, k)),
                  pl.BlockSpec((1, tn), lambda k: (0, k))],
        out_specs=pl.BlockSpec((rows, tn), lambda k: (0, k)),
        out_shape=jax.ShapeDtypeStruct((rows, n), F32),
        compiler_params=_params(("arbitrary",)),
        name="ada",
    )(c_all, w, b)


def _inproj_kernel(x_ref, shift_ref, scale_ref, w_ref, cos_ref, sin_ref,
                   qkv_ref, ksb_ref, vsb_ref, kd_ref, vd_ref, vdt_ref):
    x = x_ref[0]
    h = x * (1.0 + scale_ref[0]) + shift_ref[0]
    y = _dot(h.astype(BF16), w_ref[...])
    tm = y.shape[0]
    cos = cos_ref[...]
    sin = sin_ref[...]
    lane = lax.broadcasted_iota(jnp.int32, (tm, LANES), 1)
    first_half = (lane & (HEAD_DIM - 1)) < (HEAD_DIM // 2)

    def rope(t):
        rot = jnp.where(first_half, pltpu.roll(t, LANES - HEAD_DIM // 2, 1),
                        pltpu.roll(t, HEAD_DIM // 2, 1))
        return t * cos + rot * sin

    w = SB_WIDTH
    qkv_ref[0, :, 0:w] = (y[:, 0:w] * QK_SCALE).astype(BF16)
    ksb = y[:, w:2 * w]
    vsb = y[:, 2 * w:3 * w]
    ksb_ref[0] = ksb
    vsb_ref[0] = vsb
    qkv_ref[0, :, w:2 * w] = ksb.astype(BF16)
    qkv_ref[0, :, 2 * w:3 * w] = vsb.astype(BF16)
    for g in range(w // LANES):
        lo = 3 * w + g * LANES
        qkv_ref[0, :, lo:lo + LANES] = (rope(y[:, lo:lo + LANES]) * QK_SCALE).astype(BF16)
        lo = 4 * w + g * LANES
        kd = rope(y[:, lo:lo + LANES])
        kd_ref[0, :, g * LANES:(g + 1) * LANES] = kd
        qkv_ref[0, :, lo:lo + LANES] = kd.astype(BF16)
    vd = y[:, 5 * w:6 * w]
    vd_ref[0] = vd
    qkv_ref[0, :, 5 * w:6 * w] = vd.astype(BF16)
    vdt_ref[0] = vd.T.astype(BF16)


def _mod_spec(mod, tm, col):
    if mod.shape[1] == 1:
        return pl.BlockSpec((1, 1, D_MODEL), lambda b, i: (b, 0, col))
    return pl.BlockSpec((1, tm, D_MODEL), lambda b, i: (b, i, col))


def _inproj(x, mod, w_in, cos, sin, tm):
    g, rows, _ = x.shape
    row_spec = lambda width: pl.BlockSpec((1, tm, width), lambda b, i: (b, i, 0))
    return pl.pallas_call(
        _inproj_kernel,
        grid=(g, rows // tm),
        in_specs=[row_spec(D_MODEL), _mod_spec(mod, tm, 0), _mod_spec(mod, tm, 1),
                  pl.BlockSpec((D_MODEL, IN_WIDTH), lambda b, i: (0, 0)),
                  pl.BlockSpec((tm, LANES), lambda b, i: (i, 0)),
                  pl.BlockSpec((tm, LANES), lambda b, i: (i, 0))],
        out_specs=[row_spec(IN_WIDTH)] + [row_spec(SB_WIDTH)] * 4
                  + [pl.BlockSpec((1, SB_WIDTH, tm), lambda b, i: (b, 0, i))],
        out_shape=[jax.ShapeDtypeStruct((g, rows, IN_WIDTH), BF16)]
                  + [jax.ShapeDtypeStruct((g, rows, SB_WIDTH), F32)] * 4
                  + [jax.ShapeDtypeStruct((g, SB_WIDTH, rows), BF16)],
        compiler_params=_params(("arbitrary", "arbitrary")),
        name="inproj",
    )(x, mod, mod, w_in, cos, sin)


def _log_gates(z):
    sp = jnp.log(1.0 + jnp.exp(-jnp.abs(z)))
    l = jnp.minimum(-z, 0.0) - sp
    return l, l + z


def _triangle(n):
    j = np.arange(n)
    tri = (j[:, None] > j[None, :]).astype(np.float32)
    return jnp.asarray(np.concatenate([tri, tri], axis=0), BF16)


def _tail_sums(l, u2):
    l_hi = l.astype(BF16)
    l_lo = (l - l_hi.astype(F32)).astype(BF16)
    return _dot(jnp.concatenate([l_hi, l_lo], axis=1), u2)


def _lambda_value(lq1, lk1, lq2, lk2):
    return (jnp.exp(jnp.sum(lq1 * lk1, axis=1, keepdims=True))
            - jnp.exp(jnp.sum(lq2 * lk2, axis=1, keepdims=True)) + LAM_INIT)


def _sub_rms_norm(o, g):
    ms = jnp.mean(o * o, axis=1, keepdims=True)
    return o * lax.rsqrt(ms + LN_EPS) * g * (1.0 - LAM_INIT)


SB_DEAD = -120.0
SB_HEADS_PER_STEP = 8
DIFF_HEADS_PER_STEP = 4


def _tri_tables(nq, descending):
    qs, ks = [], []
    for qi in range(nq):
        order = range(qi, -1, -1) if descending else range(qi + 1)
        for kj in order:
            qs.append(qi)
            ks.append(kj)
    return jnp.asarray(np.array(qs, np.int32)), jnp.asarray(np.array(ks, np.int32))


def _sb_attn_kernel(qi_ref, kj_ref, q_ref, k_ref, v_ref, u_ref, o_ref, acc_ref, carry_ref, live_ref):
    t = pl.program_id(2)
    qi = qi_ref[t]
    kj = kj_ref[t]
    tq = q_ref.shape[1]

    @pl.when(kj == qi)
    def _():
        acc_ref[...] = jnp.zeros_like(acc_ref)
        carry_ref[...] = jnp.zeros_like(carry_ref)
        live_ref[0] = 1

    def step(diag):
        q = q_ref[0]
        k = k_ref[0]
        v = v_ref[0]
        if diag:
            row = lax.broadcasted_iota(jnp.int32, (tq, tq), 0)
            col = lax.broadcasted_iota(jnp.int32, (tq, tq), 1)
            mask = col < row
        heads = range(SB_HEADS_PER_STEP)
        sl = [slice(h * HEAD_DIM, (h + 1) * HEAD_DIM) for h in heads]
        gates = [_log_gates(_dot_nt(q[:, sl[h]], k[:, sl[h]])) for h in heads]
        ls = [g[1] for g in gates]
        l = [jnp.where(mask, g[0], 0.0) if diag else g[0] for g in gates]
        tails = _tail_sums(jnp.concatenate(l, axis=0), u_ref[...])
        for h in heads:
            tail = tails[h * tq:(h + 1) * tq]
            carry = carry_ref[h]
            w = jnp.exp(ls[h] + tail + carry)
            if diag:
                w = jnp.where(mask, w, 0.0)
            acc_ref[h] += _dot(w.astype(BF16), v[:, sl[h]])
            carry_ref[h] = carry + tail[:, 0:1] + l[h][:, 0:1]
        live_ref[0] = (jnp.max(carry_ref[...]) > SB_DEAD).astype(jnp.int32)

    pl.when(kj == qi)(functools.partial(step, True))
    pl.when((kj != qi) & (live_ref[0] == 1))(functools.partial(step, False))

    @pl.when(kj == 0)
    def _():
        o_ref[0] = jnp.concatenate([acc_ref[h] for h in range(SB_HEADS_PER_STEP)], axis=1).astype(o_ref.dtype)


def _sb_attn(qkv, tq):
    b, s, _ = qkv.shape
    qi_tab, kj_tab = _tri_tables(s // tq, descending=True)
    nh = SB_HEADS_PER_STEP
    width = nh * HEAD_DIM
    groups = SB_WIDTH // width
    col = lambda first: first * LANES // width
    grid_spec = pltpu.PrefetchScalarGridSpec(
        num_scalar_prefetch=2,
        grid=(b, groups, int(qi_tab.shape[0])),
        in_specs=[pl.BlockSpec((1, tq, width), lambda bb, g, t, qt, kt: (bb, qt[t], col(COL_SBQ) + g)),
                  pl.BlockSpec((1, tq, width), lambda bb, g, t, qt, kt: (bb, kt[t], col(COL_SBK) + g)),
                  pl.BlockSpec((1, tq, width), lambda bb, g, t, qt, kt: (bb, kt[t], col(COL_SBV) + g)),
                  pl.BlockSpec((2 * tq, tq), lambda bb, g, t, qt, kt: (0, 0))],
        out_specs=pl.BlockSpec((1, tq, width), lambda bb, g, t, qt, kt: (bb, qt[t], g)),
        scratch_shapes=[pltpu.VMEM((nh, tq, HEAD_DIM), F32), pltpu.VMEM((nh, tq, 1), F32),
                        pltpu.SMEM((1,), jnp.int32)],
    )
    return pl.pallas_call(
        _sb_attn_kernel,
        grid_spec=grid_spec,
        out_shape=jax.ShapeDtypeStruct((b, s, SB_WIDTH), BF16),
        compiler_params=_params(("arbitrary", "arbitrary", "arbitrary")),
        name="sb_attn",
    )(qi_tab, kj_tab, qkv, qkv, qkv, _triangle(tq))


def _diff_attn_kernel(qi_ref, kj_ref, q_ref, k_ref, vt_ref, lq1_ref, lk1_ref, lq2_ref, lk2_ref, g_ref,
                      o_ref, acc_ref, m_ref, l_ref):
    t = pl.program_id(2)
    qi = qi_ref[t]
    kj = kj_ref[t]
    tq = q_ref.shape[1]

    @pl.when(kj == 0)
    def _():
        acc_ref[...] = jnp.zeros_like(acc_ref)
        l_ref[...] = jnp.zeros_like(l_ref)
        m_ref[...] = jnp.full_like(m_ref, NEG)

    def step(diag):
        q = q_ref[0]
        k = k_ref[0]
        vt = vt_ref[0]
        if diag:
            key = lax.broadcasted_iota(jnp.int32, (tq, tq), 0)
            qry = lax.broadcasted_iota(jnp.int32, (tq, tq), 1)
            mask = key <= qry
        maps = range(2 * DIFF_HEADS_PER_STEP)
        sl = [slice(i * HEAD_DIM, (i + 1) * HEAD_DIM) for i in maps]
        s = [_dot_nt(k[:, sl[i]], q[:, sl[i]]) for i in maps]
        if diag:
            s = [jnp.where(mask, si, NEG) for si in s]
        m_prev = [m_ref[i] for i in maps]
        m_new = [jnp.maximum(m_prev[i], jnp.max(s[i], axis=0, keepdims=True)) for i in maps]
        alpha = [jnp.exp(m_prev[i] - m_new[i]) for i in maps]
        p = [jnp.exp(s[i] - m_new[i]) for i in maps]
        for i in maps:
            vh = vt[(i // 2) * DIFF_V_DIM:(i // 2 + 1) * DIFF_V_DIM, :]
            l_ref[i] = alpha[i] * l_ref[i] + jnp.sum(p[i], axis=0, keepdims=True)
            acc_ref[i] = alpha[i] * acc_ref[i] + _dot(vh, p[i].astype(BF16))
            m_ref[i] = m_new[i]

    pl.when(kj == qi)(functools.partial(step, True))
    pl.when(kj != qi)(functools.partial(step, False))

    @pl.when(kj == qi)
    def _():
        lam = _lambda_value(lq1_ref[...], lk1_ref[...], lq2_ref[...], lk2_ref[...])
        for hv in range(DIFF_HEADS_PER_STEP):
            o = acc_ref[2 * hv] / l_ref[2 * hv] - lam * (acc_ref[2 * hv + 1] / l_ref[2 * hv + 1])
            ms = jnp.mean(o * o, axis=0, keepdims=True)
            on = o * lax.rsqrt(ms + LN_EPS) * g_ref[...] * (1.0 - LAM_INIT)
            o_ref[0, :, hv * DIFF_V_DIM:(hv + 1) * DIFF_V_DIM] = on.T.astype(o_ref.dtype)


def _diff_attn(qkv, vdt, lams, subln_g, tq):
    b, s, _ = qkv.shape
    qi_tab, kj_tab = _tri_tables(s // tq, descending=False)
    nhv = DIFF_HEADS_PER_STEP
    width = nhv * DIFF_V_DIM
    groups = DIFF_HEADS // nhv
    col = lambda first: first * LANES // width
    small = lambda w: pl.BlockSpec((1, w), lambda bb, g, t, qt, kt: (0, 0))
    grid_spec = pltpu.PrefetchScalarGridSpec(
        num_scalar_prefetch=2,
        grid=(b, groups, int(qi_tab.shape[0])),
        in_specs=[pl.BlockSpec((1, tq, width), lambda bb, g, t, qt, kt: (bb, qt[t], col(COL_DQ) + g)),
                  pl.BlockSpec((1, tq, width), lambda bb, g, t, qt, kt: (bb, kt[t], col(COL_DK) + g)),
                  pl.BlockSpec((1, width, tq), lambda bb, g, t, qt, kt: (bb, g, kt[t])),
                  small(HEAD_DIM), small(HEAD_DIM), small(HEAD_DIM), small(HEAD_DIM),
                  pl.BlockSpec((DIFF_V_DIM, 1), lambda bb, g, t, qt, kt: (0, 0))],
        out_specs=pl.BlockSpec((1, tq, width), lambda bb, g, t, qt, kt: (bb, qt[t], g)),
        scratch_shapes=[pltpu.VMEM((2 * nhv, DIFF_V_DIM, tq), F32), pltpu.VMEM((2 * nhv, 1, tq), F32),
                        pltpu.VMEM((2 * nhv, 1, tq), F32)],
    )
    return pl.pallas_call(
        _diff_attn_kernel,
        grid_spec=grid_spec,
        out_shape=jax.ShapeDtypeStruct((b, s, SB_WIDTH), BF16),
        compiler_params=_params(("arbitrary", "arbitrary", "arbitrary")),
        name="diff_attn",
    )(qi_tab, kj_tab, qkv, qkv, vdt, *lams, subln_g.reshape(DIFF_V_DIM, 1))


DEC_PAGES_PER_STEP = 8
DEC_ROWS = 64


def _decode_kernel(pt_ref, qs_ref, qd_ref, ksn_ref, vsn_ref, kdn_ref, vdn_ref, *rest):
    npg = DEC_PAGES_PER_STEP
    pages = rest[:4 * npg]
    (u_ref, lq1_ref, lk1_ref, lq2_ref, lk2_ref, g_ref, so_ref, do_ref,
     sacc_ref, scarry_ref, dacc_ref, dm_ref, dl_ref, pad_ref) = rest[4 * npg:]
    j = pl.program_id(1)
    qs = qs_ref[0]
    qd = qd_ref[0]
    n_new = ksn_ref.shape[1]

    def sb_tile(z, valid, pv):
        l, ls = _log_gates(z)
        if valid is not None:
            l = jnp.where(valid, l, 0.0)
        n = z.shape[1] // PAGE_SIZE
        chunk = lambda a, p: a[:, p * PAGE_SIZE:(p + 1) * PAGE_SIZE]
        tails = _tail_sums(jnp.concatenate([chunk(l, p) for p in range(n)], axis=0), u_ref[...])
        carry = scarry_ref[...]
        ws = []
        for p in range(n):
            tail = tails[p * DEC_ROWS:(p + 1) * DEC_ROWS]
            ws.append(jnp.exp(chunk(ls, p) + tail + carry))
            carry = carry + tail[:, 0:1] + chunk(l, p)[:, 0:1]
        w = ws[0] if n == 1 else jnp.concatenate(ws, axis=1)
        if valid is not None:
            w = jnp.where(valid, w, 0.0)
        sacc_ref[...] += pv(w.astype(BF16))
        scarry_ref[...] = carry

    def diff_tile(s, valid, v_of):
        if valid is not None:
            s = jnp.where(valid, s, NEG)
        m_prev = dm_ref[...]
        m_new = jnp.maximum(m_prev, jnp.max(s, axis=1, keepdims=True))
        alpha = jnp.exp(m_prev - m_new)
        p = jnp.exp(s - m_new)
        if valid is not None:
            p = jnp.where(valid, p, 0.0)
        dl_ref[...] = alpha * dl_ref[...] + jnp.sum(p, axis=1, keepdims=True)
        dm_ref[...] = m_new
        pb = p.astype(BF16)
        for hv in range(DIFF_HEADS):
            rows = slice(16 * hv, 16 * hv + 16)
            dacc_ref[rows, :] = alpha[rows] * dacc_ref[rows, :] + _dot(pb[rows], v_of(hv))

    @pl.when(j == 0)
    def _():
        sacc_ref[...] = jnp.zeros_like(sacc_ref)
        scarry_ref[...] = jnp.zeros_like(scarry_ref)
        dacc_ref[...] = jnp.zeros_like(dacc_ref)
        dl_ref[...] = jnp.zeros_like(dl_ref)
        dm_ref[...] = jnp.full_like(dm_ref, NEG)
        pad_ref[...] = jnp.zeros_like(pad_ref)
        pad_ref[0, 0:n_new, :] = ksn_ref[0]
        pad_ref[1, 0:n_new, :] = vsn_ref[0]
        pad_ref[2, 0:n_new, :] = kdn_ref[0]
        pad_ref[3, 0:n_new, :] = vdn_ref[0]
        tok = lax.broadcasted_iota(jnp.int32, (DEC_ROWS, PAGE_SIZE), 1)
        qpos = lax.broadcasted_iota(jnp.int32, (DEC_ROWS, PAGE_SIZE), 0) & (n_new - 1)
        vn = pad_ref[1].astype(BF16)
        sb_tile(_dot_nt(qs, pad_ref[0].astype(BF16)), tok < qpos, lambda w: _dot(w, vn))
        vdn = pad_ref[3].astype(BF16)
        diff_tile(_dot_nt(qd, pad_ref[2].astype(BF16)), tok <= qpos,
                  lambda hv: vdn[:, hv * DIFF_V_DIM:(hv + 1) * DIFF_V_DIM])

    side_by_side = lambda which: jnp.concatenate(
        [pages[4 * p + which][...] for p in range(npg)], axis=1).astype(BF16)
    sb_tile(_dot(qs, side_by_side(0)), None, lambda w: _dot_nt(w, side_by_side(1)))
    diff_tile(_dot(qd, side_by_side(2)), None,
              lambda hv: jnp.concatenate(
                  [pages[4 * p + 3][pl.ds(hv, PAGE_SIZE, stride=DIFF_HEADS), :] for p in range(npg)],
                  axis=0).astype(BF16))

    @pl.when(j == pl.num_programs(1) - 1)
    def _():
        for h in range(SB_HEADS):
            so_ref[0, 8 * h:8 * h + 8, :] = sacc_ref[8 * h:8 * h + 8, h * HEAD_DIM:(h + 1) * HEAD_DIM]
        on = dacc_ref[...] / dl_ref[...]
        lam = _lambda_value(lq1_ref[...], lk1_ref[...], lq2_ref[...], lk2_ref[...])
        for hv in range(DIFF_HEADS):
            o = on[16 * hv:16 * hv + 8] - lam * on[16 * hv + 8:16 * hv + 16]
            do_ref[0, 8 * hv:8 * hv + 8, :] = _sub_rms_norm(o, g_ref[...])


def _decode(page_table, qs, qd, ksn, vsn, kdn, vdn, csk, csv, cdk, cdv, lams, subln_g):
    n_seq, n_pages = page_table.shape
    npg = DEC_PAGES_PER_STEP
    assert n_pages % npg == 0, (n_pages, npg)
    n_new = ksn.shape[1]
    rows = csk.shape[1]
    seq_spec = lambda r, w: pl.BlockSpec((1, r, w), lambda s, j, pt: (s, 0, 0))
    small = lambda w: pl.BlockSpec((1, w), lambda s, j, pt: (0, 0))

    def page_spec(p):
        return pl.BlockSpec((None, rows, PAGE_SIZE), lambda s, j, pt: (pt[s, n_pages - 1 - (j * npg + p)], 0, 0))

    page_specs, page_args = [], []
    for p in range(npg):
        page_specs += [page_spec(p)] * 4
        page_args += [csk, csv, cdk, cdv]
    grid_spec = pltpu.PrefetchScalarGridSpec(
        num_scalar_prefetch=1,
        grid=(n_seq, n_pages // npg),
        in_specs=[seq_spec(DEC_ROWS, rows), seq_spec(DEC_ROWS, rows)] + [seq_spec(n_new, rows)] * 4
                 + page_specs
                 + [pl.BlockSpec((2 * PAGE_SIZE, PAGE_SIZE), lambda s, j, pt: (0, 0)),
                    small(HEAD_DIM), small(HEAD_DIM), small(HEAD_DIM), small(HEAD_DIM), small(DIFF_V_DIM)],
        out_specs=[seq_spec(DEC_ROWS, HEAD_DIM), seq_spec(DEC_ROWS // 2, DIFF_V_DIM)],
        scratch_shapes=[pltpu.VMEM((DEC_ROWS, rows), F32), pltpu.VMEM((DEC_ROWS, 1), F32),
                        pltpu.VMEM((DEC_ROWS, DIFF_V_DIM), F32), pltpu.VMEM((DEC_ROWS, 1), F32),
                        pltpu.VMEM((DEC_ROWS, 1), F32), pltpu.VMEM((4, PAGE_SIZE, rows), F32)],
    )
    return pl.pallas_call(
        _decode_kernel,
        grid_spec=grid_spec,
        out_shape=[jax.ShapeDtypeStruct((n_seq, DEC_ROWS, HEAD_DIM), F32),
                   jax.ShapeDtypeStruct((n_seq, DEC_ROWS // 2, DIFF_V_DIM), F32)],
        compiler_params=_params(("arbitrary", "arbitrary")),
        name="decode",
    )(page_table, qs, qd, ksn, vsn, kdn, vdn, *page_args, _triangle(PAGE_SIZE), *lams, subln_g)


def _layer_norm(y, g, b):
    mu = jnp.mean(y, axis=1, keepdims=True)
    d = y - mu
    var = jnp.mean(d * d, axis=1, keepdims=True)
    return d * lax.rsqrt(var + LN_EPS) * g + b


def _merge_kernel(x_ref, sb_ref, d_ref, gate_ref, w_ref, g_ref, b_ref, o_ref):
    o = _dot(sb_ref[0], w_ref[0:SB_WIDTH, :]) + _dot(d_ref[0], w_ref[SB_WIDTH:, :])
    y = DEEPNORM_ALPHA * x_ref[0] + gate_ref[0] * o
    o_ref[0] = _layer_norm(y, g_ref[...], b_ref[...])


def _merge(x, sb_o, d_o, mod, w_out, g, b, tm):
    grp, rows, _ = x.shape
    row_spec = lambda width: pl.BlockSpec((1, tm, width), lambda bb, i: (bb, i, 0))
    vec = pl.BlockSpec((1, D_MODEL), lambda bb, i: (0, 0))
    return pl.pallas_call(
        _merge_kernel,
        grid=(grp, rows // tm),
        in_specs=[row_spec(D_MODEL), row_spec(SB_WIDTH), row_spec(SB_WIDTH), _mod_spec(mod, tm, 2),
                  pl.BlockSpec((D_MODEL, D_MODEL), lambda bb, i: (0, 0)), vec, vec],
        out_specs=row_spec(D_MODEL),
        out_shape=jax.ShapeDtypeStruct(x.shape, F32),
        compiler_params=_params(("arbitrary", "arbitrary")),
        name="merge",
    )(x, sb_o, d_o, mod, w_out, g, b)


def _ffn_kernel(x_ref, shift_ref, scale_ref, gate_ref, wg_ref, wu_ref, wd_ref, g_ref, b_ref, o_ref, acc_ref):
    c = pl.program_id(2)
    x = x_ref[0]
    h = (x * (1.0 + scale_ref[0]) + shift_ref[0]).astype(BF16)
    gt = _dot(h, wg_ref[...])
    up = _dot(h, wu_ref[...])
    f = (gt * (1.0 / (1.0 + jnp.exp(-gt))) * up).astype(BF16)
    part = _dot(f, wd_ref[...])

    @pl.when(c == 0)
    def _():
        acc_ref[...] = part

    @pl.when(c != 0)
    def _():
        acc_ref[...] += part

    @pl.when(c == pl.num_programs(2) - 1)
    def _():
        y = DEEPNORM_ALPHA * x + gate_ref[0] * acc_ref[...]
        o_ref[0] = _layer_norm(y, g_ref[...], b_ref[...])


def _ffn(x, mod, w_gu, w_dn, g, b, tm, n_chunks):
    grp, rows, _ = x.shape
    fc = D_FF // n_chunks
    row_spec = pl.BlockSpec((1, tm, D_MODEL), lambda bb, i, c: (bb, i, 0))
    vec = pl.BlockSpec((1, D_MODEL), lambda bb, i, c: (0, 0))

    def mod_spec(col):
        if mod.shape[1] == 1:
            return pl.BlockSpec((1, 1, D_MODEL), lambda bb, i, c: (bb, 0, col))
        return pl.BlockSpec((1, tm, D_MODEL), lambda bb, i, c: (bb, i, col))

    return pl.pallas_call(
        _ffn_kernel,
        grid=(grp, rows // tm, n_chunks),
        in_specs=[row_spec, mod_spec(3), mod_spec(4), mod_spec(5),
                  pl.BlockSpec((D_MODEL, fc), lambda bb, i, c: (0, c)),
                  pl.BlockSpec((D_MODEL, fc), lambda bb, i, c: (0, n_chunks + c)),
                  pl.BlockSpec((fc, D_MODEL), lambda bb, i, c: (c, 0)), vec, vec],
        out_specs=row_spec,
        out_shape=jax.ShapeDtypeStruct(x.shape, F32),
        scratch_shapes=[pltpu.VMEM((tm, D_MODEL), F32)],
        compiler_params=_params(("arbitrary", "arbitrary", "arbitrary")),
        name="ffn",
    )(x, mod, mod, mod, w_gu, w_gu, w_dn, g, b)


def _rope_tables(pos):
    half = HEAD_DIM // 2
    inv = ROPE_THETA ** (-jnp.arange(half, dtype=F32) / half)
    ang = pos.astype(F32)[:, None] * inv[None, :]
    cos = jnp.cos(ang)
    sin = jnp.sin(ang)
    return jnp.concatenate([cos] * 4, axis=1), jnp.concatenate([-sin, sin, -sin, sin], axis=1)


def kernel(x_prompt, x_sample, cache_sb_k, cache_sb_v, cache_diff_k, cache_diff_v, page_table, c_prompt, c_sample, w_ada, b_ada, w_in, w_out, lambda_q1, lambda_k1, lambda_q2, lambda_k2, subln_g, ln1_g, ln1_b, w_gate_up, w_down, ln2_g, ln2_b):
    n_batch, seq, _ = x_prompt.shape
    n_dec, n_new, _ = x_sample.shape
    n_pool = cache_sb_k.shape[0]
    lyr = 0

    n_c = n_batch + n_dec
    pad = (-n_c) % 8
    c_all = jnp.concatenate([c_prompt, c_sample, jnp.zeros((pad, D_MODEL), F32)], axis=0)
    mod = _ada(c_all, w_ada[lyr], b_ada[lyr][None, :])
    mod_p = mod[:n_batch].reshape(n_batch, 1, 6 * D_MODEL)
    mod_s = jnp.repeat(mod[n_batch:n_c], n_new, axis=0).reshape(1, n_dec * n_new, 6 * D_MODEL)

    w_in_b = w_in[lyr].astype(BF16)
    w_out_b = w_out[lyr].astype(BF16)
    w_gu_b = w_gate_up[lyr].astype(BF16)
    w_dn_b = w_down[lyr].astype(BF16)
    lams = (lambda_q1, lambda_k1, lambda_q2, lambda_k2)
    g1, b1, g2, b2 = ln1_g[lyr][None, :], ln1_b[lyr][None, :], ln2_g[lyr][None, :], ln2_b[lyr][None, :]

    cos_p, sin_p = _rope_tables(jnp.arange(seq, dtype=jnp.int32))
    qkv_p, ksb_p, vsb_p, kd_p, vd_p, vdt_p = _inproj(x_prompt, mod_p, w_in_b, cos_p, sin_p, tm=512)
    sb_o_p = _sb_attn(qkv_p, tq=256)
    d_o_p = _diff_attn(qkv_p, vdt_p, lams, subln_g, tq=256)
    x1_p = _merge(x_prompt, sb_o_p, d_o_p, mod_p, w_out_b, g1, b1, tm=512)
    y_p = _ffn(x1_p, mod_p, w_gu_b, w_dn_b, g2, b2, tm=512, n_chunks=2)

    rows_s = n_dec * n_new
    cos_n, sin_n = _rope_tables(PAST_LEN + jnp.arange(n_new, dtype=jnp.int32))
    cos_s = jnp.tile(cos_n, (n_dec, 1))
    sin_s = jnp.tile(sin_n, (n_dec, 1))
    xs = x_sample.reshape(1, rows_s, D_MODEL)
    qkv_s, ksb_s, vsb_s, kd_s, vd_s, _ = _inproj(xs, mod_s, w_in_b, cos_s, sin_s, tm=256)

    w = SB_WIDTH

    def block_diag_rows(q):
        q = q.reshape(n_dec, n_new, SB_HEADS, HEAD_DIM).transpose(0, 2, 1, 3)
        same = jnp.eye(SB_HEADS, dtype=bool)[None, :, None, :, None]
        return jnp.where(same, q[:, :, :, None, :], 0).reshape(n_dec, SB_HEADS * n_new, w)

    def token_minor(cache):
        return cache.transpose(0, 1, 3, 4, 2).reshape(n_pool, w, PAGE_SIZE)

    so, do = _decode(
        page_table, block_diag_rows(qkv_s[0, :, 0:w]), block_diag_rows(qkv_s[0, :, 3 * w:4 * w]),
        ksb_s.reshape(n_dec, n_new, w), vsb_s.reshape(n_dec, n_new, w),
        kd_s.reshape(n_dec, n_new, w), vd_s.reshape(n_dec, n_new, w),
        token_minor(cache_sb_k), token_minor(cache_sb_v), token_minor(cache_diff_k),
        cache_diff_v.reshape(n_pool, PAGE_SIZE * DIFF_HEADS, DIFF_V_DIM),
        lams, subln_g)
    sb_o_s = so.reshape(n_dec, SB_HEADS, n_new, HEAD_DIM).transpose(0, 2, 1, 3).reshape(1, rows_s, w)
    d_o_s = do.reshape(n_dec, DIFF_HEADS, n_new, DIFF_V_DIM).transpose(0, 2, 1, 3).reshape(1, rows_s, w)
    x1_s = _merge(xs, sb_o_s.astype(BF16), d_o_s.astype(BF16), mod_s, w_out_b, g1, b1, tm=256)
    y_s = _ffn(x1_s, mod_s, w_gu_b, w_dn_b, g2, b2, tm=256, n_chunks=2)

    return (y_p, y_s.reshape(n_dec, n_new, D_MODEL),
            ksb_p.reshape(n_batch, 1, seq, SB_HEADS, HEAD_DIM),
            vsb_p.reshape(n_batch, 1, seq, SB_HEADS, HEAD_DIM),
            kd_p.reshape(n_batch, 1, seq, 2 * DIFF_HEADS, HEAD_DIM),
            vd_p.reshape(n_batch, 1, seq, DIFF_HEADS, DIFF_V_DIM),
            ksb_s.reshape(n_dec, 1, n_new, SB_HEADS, HEAD_DIM),
            vsb_s.reshape(n_dec, 1, n_new, SB_HEADS, HEAD_DIM),
            kd_s.reshape(n_dec, 1, n_new, 2 * DIFF_HEADS, HEAD_DIM),
            vd_s.reshape(n_dec, 1, n_new, DIFF_HEADS, DIFF_V_DIM))
```

```python
import functools
import math

import numpy as np
import jax
import jax.numpy as jnp
from jax import lax
from jax.experimental import pallas as pl
from jax.experimental.pallas import tpu as pltpu

F32 = jnp.float32
BF16 = jnp.bfloat16

D_MODEL = 1024
HEAD_DIM = 64
SB_HEADS = 8
DIFF_HEADS = 4
SB_WIDTH = SB_HEADS * HEAD_DIM
DIFF_V_DIM = 2 * HEAD_DIM
IN_WIDTH = 6 * SB_WIDTH
D_FF = 2816
PAGE_SIZE = 128
PAST_LEN = 8192
ROPE_THETA = 10000.0
LN_EPS = 1e-5
DEPTH = 1
DEEPNORM_ALPHA = (2 * DEPTH) ** 0.25
LAM_INIT = 0.8 - 0.6 * math.exp(-0.3 * 0)
QK_SCALE = HEAD_DIM ** -0.5
NEG = -1e30

LANES = 128
VMEM_LIMIT = 56 * 1024 * 1024

COL_SBQ, COL_SBK, COL_SBV, COL_DQ, COL_DK, COL_DV = 0, 4, 8, 12, 16, 20

NT_DIMS = (((1,), (1,)), ((), ()))


def _dot(a, b):
    return jnp.dot(a, b, preferred_element_type=F32)


def _dot_nt(a, b):
    return lax.dot_general(a, b, NT_DIMS, preferred_element_type=F32)


def _params(sem):
    return pltpu.CompilerParams(dimension_semantics=sem, vmem_limit_bytes=VMEM_LIMIT)


def _ada_kernel(c_ref, w_ref, b_ref, o_ref):
    c = c_ref[...]
    a = c * (1.0 / (1.0 + jnp.exp(-c)))
    o_ref[...] = jnp.dot(a, w_ref[...], preferred_element_type=F32,
                         precision=lax.Precision.HIGHEST) + b_ref[...]


def _ada(c_all, w, b):
    rows = c_all.shape[0]
    n = w.shape[1]
    tn = 1024
    return pl.pallas_call(
        _ada_kernel,
        grid=(n // tn,),
        in_specs=[pl.BlockSpec((rows, D_MODEL), lambda k: (0, 0)),
                  pl.BlockSpec((D_MODEL, tn), lambda k: (0, k)),
                  pl.BlockSpec((1, tn), lambda k: (0, k))],
        out_specs=pl.BlockSpec((rows, tn), lambda k: (0, k)),
        out_shape=jax.ShapeDtypeStruct((rows, n), F32),
        compiler_params=_params(("arbitrary",)),
        name="ada",
    )(c_all, w, b)


def _inproj_kernel(x_ref, shift_ref, scale_ref, w_ref, cos_ref, sin_ref,
                   qkv_ref, ksb_ref, vsb_ref, kd_ref, vd_ref, vdt_ref):
    x = x_ref[0]
    h = x * (1.0 + scale_ref[0]) + shift_ref[0]
    y = _dot(h.astype(BF16), w_ref[...])
    tm = y.shape[0]
    cos = cos_ref[...]
    sin = sin_ref[...]
    lane = lax.broadcasted_iota(jnp.int32, (tm, LANES), 1)
    first_half = (lane & (HEAD_DIM - 1)) < (HEAD_DIM // 2)

    def rope(t):
        rot = jnp.where(first_half, pltpu.roll(t, LANES - HEAD_DIM // 2, 1),
                        pltpu.roll(t, HEAD_DIM // 2, 1))
        return t * cos + rot * sin

    w = SB_WIDTH
    qkv_ref[0, :, 0:w] = (y[:, 0:w] * QK_SCALE).astype(BF16)
    ksb = y[:, w:2 * w]
    vsb = y[:, 2 * w:3 * w]
    ksb_ref[0] = ksb
    vsb_ref[0] = vsb
    qkv_ref[0, :, w:2 * w] = ksb.astype(BF16)
    qkv_ref[0, :, 2 * w:3 * w] = vsb.astype(BF16)
    for g in range(w // LANES):
        lo = 3 * w + g * LANES
        qkv_ref[0, :, lo:lo + LANES] = (rope(y[:, lo:lo + LANES]) * QK_SCALE).astype(BF16)
        lo = 4 * w + g * LANES
        kd = rope(y[:, lo:lo + LANES])
        kd_ref[0, :, g * LANES:(g + 1) * LANES] = kd
        qkv_ref[0, :, lo:lo + LANES] = kd.astype(BF16)
    vd = y[:, 5 * w:6 * w]
    vd_ref[0] = vd
    qkv_ref[0, :, 5 * w:6 * w] = vd.astype(BF16)
    vdt_ref[0] = vd.T.astype(BF16)


def _mod_spec(mod, tm, col):
    if mod.shape[1] == 1:
        return pl.BlockSpec((1, 1, D_MODEL), lambda b, i: (b, 0, col))
    return pl.BlockSpec((1, tm, D_MODEL), lambda b, i: (b, i, col))


def _inproj(x, mod, w_in, cos, sin, tm):
    g, rows, _ = x.shape
    row_spec = lambda width: pl.BlockSpec((1, tm, width), lambda b, i: (b, i, 0))
    return pl.pallas_call(
        _inproj_kernel,
        grid=(g, rows // tm),
        in_specs=[row_spec(D_MODEL), _mod_spec(mod, tm, 0), _mod_spec(mod, tm, 1),
                  pl.BlockSpec((D_MODEL, IN_WIDTH), lambda b, i: (0, 0)),
                  pl.BlockSpec((tm, LANES), lambda b, i: (i, 0)),
                  pl.BlockSpec((tm, LANES), lambda b, i: (i, 0))],
        out_specs=[row_spec(IN_WIDTH)] + [row_spec(SB_WIDTH)] * 4
                  + [pl.BlockSpec((1, SB_WIDTH, tm), lambda b, i: (b, 0, i))],
        out_shape=[jax.ShapeDtypeStruct((g, rows, IN_WIDTH), BF16)]
                  + [jax.ShapeDtypeStruct((g, rows, SB_WIDTH), F32)] * 4
                  + [jax.ShapeDtypeStruct((g, SB_WIDTH, rows), BF16)],
        compiler_params=_params(("arbitrary", "arbitrary")),
        name="inproj",
    )(x, mod, mod, w_in, cos, sin)


def _log_gates(z):
    sp = jnp.log(1.0 + jnp.exp(-jnp.abs(z)))
    l = jnp.minimum(-z, 0.0) - sp
    return l, l + z


def _triangle(n):
    j = np.arange(n)
    tri = (j[:, None] > j[None, :]).astype(np.float32)
    return jnp.asarray(np.concatenate([tri, tri], axis=0), BF16)


def _tail_sums(l, u2):
    l_hi = l.astype(BF16)
    l_lo = (l - l_hi.astype(F32)).astype(BF16)
    return _dot(jnp.concatenate([l_hi, l_lo], axis=1), u2)


def _lambda_value(lq1, lk1, lq2, lk2):
    return (jnp.exp(jnp.sum(lq1 * lk1, axis=1, keepdims=True))
            - jnp.exp(jnp.sum(lq2 * lk2, axis=1, keepdims=True)) + LAM_INIT)


def _sub_rms_norm(o, g):
    ms = jnp.mean(o * o, axis=1, keepdims=True)
    return o * lax.rsqrt(ms + LN_EPS) * g * (1.0 - LAM_INIT)


SB_DEAD = -120.0
SB_HEADS_PER_STEP = 8
DIFF_HEADS_PER_STEP = 4


def _tri_tables(nq, descending):
    qs, ks = [], []
    for qi in range(nq):
        order = range(qi, -1, -1) if descending else range(qi + 1)
        for kj in order:
            qs.append(qi)
            ks.append(kj)
    return jnp.asarray(np.array(qs, np.int32)), jnp.asarray(np.array(ks, np.int32))


def _sb_attn_kernel(q_ref, k_ref, v_ref, u_ref, o_ref, acc_ref, carry_ref):
    qi = pl.program_id(1)
    tq = q_ref.shape[1]
    acc_ref[...] = jnp.zeros_like(acc_ref)
    carry_ref[...] = jnp.zeros_like(carry_ref)

    def step(kj, diag):
        q = q_ref[0]
        rows = pl.ds(pl.multiple_of(kj * tq, tq), tq)
        k = k_ref[0, rows, :]
        v = v_ref[0, rows, :]
        if diag:
            row = lax.broadcasted_iota(jnp.int32, (tq, tq), 0)
            col = lax.broadcasted_iota(jnp.int32, (tq, tq), 1)
            mask = col < row
        heads = range(SB_HEADS_PER_STEP)
        sl = [slice(h * HEAD_DIM, (h + 1) * HEAD_DIM) for h in heads]
        gates = [_log_gates(_dot_nt(q[:, sl[h]], k[:, sl[h]])) for h in heads]
        ls = [g[1] for g in gates]
        l = [jnp.where(mask, g[0], 0.0) if diag else g[0] for g in gates]
        tails = _tail_sums(jnp.concatenate(l, axis=0), u_ref[...])
        for h in heads:
            tail = tails[h * tq:(h + 1) * tq]
            carry = carry_ref[h]
            w = jnp.exp(ls[h] + tail + carry)
            if diag:
                w = jnp.where(mask, w, 0.0)
            acc_ref[h] += _dot(w.astype(BF16), v[:, sl[h]])
            carry_ref[h] = carry + tail[:, 0:1] + l[h][:, 0:1]
        return (jnp.max(carry_ref[...]) > SB_DEAD).astype(jnp.int32)

    live = step(qi, True)

    def older(state):
        kj, _ = state
        return kj - 1, step(kj, False)

    lax.while_loop(lambda state: (state[0] >= 0) & (state[1] == 1), older, (qi - 1, live))
    o_ref[0] = jnp.concatenate([acc_ref[h] for h in range(SB_HEADS_PER_STEP)], axis=1).astype(o_ref.dtype)


def _sb_attn(qkv, tq):
    b, s, _ = qkv.shape
    nh = SB_HEADS_PER_STEP
    assert nh == SB_HEADS
    col = lambda first: first * LANES // SB_WIDTH
    return pl.pallas_call(
        _sb_attn_kernel,
        grid=(b, s // tq),
        in_specs=[pl.BlockSpec((1, tq, SB_WIDTH), lambda bb, qi: (bb, qi, col(COL_SBQ))),
                  pl.BlockSpec((1, s, SB_WIDTH), lambda bb, qi: (bb, 0, col(COL_SBK)),
                               pipeline_mode=pl.Buffered(1)),
                  pl.BlockSpec((1, s, SB_WIDTH), lambda bb, qi: (bb, 0, col(COL_SBV)),
                               pipeline_mode=pl.Buffered(1)),
                  pl.BlockSpec((2 * tq, tq), lambda bb, qi: (0, 0))],
        out_specs=pl.BlockSpec((1, tq, SB_WIDTH), lambda bb, qi: (bb, qi, 0)),
        out_shape=jax.ShapeDtypeStruct((b, s, SB_WIDTH), BF16),
        scratch_shapes=[pltpu.VMEM((nh, tq, HEAD_DIM), F32), pltpu.VMEM((nh, tq, 1), F32)],
        compiler_params=_params(("arbitrary", "arbitrary")),
        name="sb_attn",
    )(qkv, qkv, qkv, _triangle(tq))


def _diff_attn_kernel(qi_ref, kj_ref, q_ref, k_ref, vt_ref, lq1_ref, lk1_ref, lq2_ref, lk2_ref, g_ref,
                      o_ref, acc_ref, m_ref, l_ref):
    t = pl.program_id(2)
    qi = qi_ref[t]
    kj = kj_ref[t]
    tq = q_ref.shape[1]

    @pl.when(kj == 0)
    def _():
        acc_ref[...] = jnp.zeros_like(acc_ref)
        l_ref[...] = jnp.zeros_like(l_ref)
        m_ref[...] = jnp.full_like(m_ref, NEG)

    def step(diag):
        q = q_ref[0]
        k = k_ref[0]
        vt = vt_ref[0]
        if diag:
            key = lax.broadcasted_iota(jnp.int32, (tq, tq), 0)
            qry = lax.broadcasted_iota(jnp.int32, (tq, tq), 1)
            mask = key <= qry
        maps = range(2 * DIFF_HEADS_PER_STEP)
        sl = [slice(i * HEAD_DIM, (i + 1) * HEAD_DIM) for i in maps]
        s = [_dot_nt(k[:, sl[i]], q[:, sl[i]]) for i in maps]
        if diag:
            s = [jnp.where(mask, si, NEG) for si in s]
        m_prev = [m_ref[i] for i in maps]
        m_new = [jnp.maximum(m_prev[i], jnp.max(s[i], axis=0, keepdims=True)) for i in maps]
        alpha = [jnp.exp(m_prev[i] - m_new[i]) for i in maps]
        p = [jnp.exp(s[i] - m_new[i]) for i in maps]
        for i in maps:
            vh = vt[(i // 2) * DIFF_V_DIM:(i // 2 + 1) * DIFF_V_DIM, :]
            l_ref[i] = alpha[i] * l_ref[i] + jnp.sum(p[i], axis=0, keepdims=True)
            acc_ref[i] = alpha[i] * acc_ref[i] + _dot(vh, p[i].astype(BF16))
            m_ref[i] = m_new[i]

    pl.when(kj == qi)(functools.partial(step, True))
    pl.when(kj != qi)(functools.partial(step, False))

    @pl.when(kj == qi)
    def _():
        lam = _lambda_value(lq1_ref[...], lk1_ref[...], lq2_ref[...], lk2_ref[...])
        for hv in range(DIFF_HEADS_PER_STEP):
            o = acc_ref[2 * hv] / l_ref[2 * hv] - lam * (acc_ref[2 * hv + 1] / l_ref[2 * hv + 1])
            ms = jnp.mean(o * o, axis=0, keepdims=True)
            on = o * lax.rsqrt(ms + LN_EPS) * g_ref[...] * (1.0 - LAM_INIT)
            o_ref[0, :, hv * DIFF_V_DIM:(hv + 1) * DIFF_V_DIM] = on.T.astype(o_ref.dtype)


def _diff_attn(qkv, vdt, lams, subln_g, tq):
    b, s, _ = qkv.shape
    qi_tab, kj_tab = _tri_tables(s // tq, descending=False)
    nhv = DIFF_HEADS_PER_STEP
    width = nhv * DIFF_V_DIM
    groups = DIFF_HEADS // nhv
    col = lambda first: first * LANES // width
    small = lambda w: pl.BlockSpec((1, w), lambda bb, g, t, qt, kt: (0, 0))
    grid_spec = pltpu.PrefetchScalarGridSpec(
        num_scalar_prefetch=2,
        grid=(b, groups, int(qi_tab.shape[0])),
        in_specs=[pl.BlockSpec((1, tq, width), lambda bb, g, t, qt, kt: (bb, qt[t], col(COL_DQ) + g)),
                  pl.BlockSpec((1, tq, width), lambda bb, g, t, qt, kt: (bb, kt[t], col(COL_DK) + g)),
                  pl.BlockSpec((1, width, tq), lambda bb, g, t, qt, kt: (bb, g, kt[t])),
                  small(HEAD_DIM), small(HEAD_DIM), small(HEAD_DIM), small(HEAD_DIM),
                  pl.BlockSpec((DIFF_V_DIM, 1), lambda bb, g, t, qt, kt: (0, 0))],
        out_specs=pl.BlockSpec((1, tq, width), lambda bb, g, t, qt, kt: (bb, qt[t], g)),
        scratch_shapes=[pltpu.VMEM((2 * nhv, DIFF_V_DIM, tq), F32), pltpu.VMEM((2 * nhv, 1, tq), F32),
                        pltpu.VMEM((2 * nhv, 1, tq), F32)],
    )
    return pl.pallas_call(
        _diff_attn_kernel,
        grid_spec=grid_spec,
        out_shape=jax.ShapeDtypeStruct((b, s, SB_WIDTH), BF16),
        compiler_params=_params(("arbitrary", "arbitrary", "arbitrary")),
        name="diff_attn",
    )(qi_tab, kj_tab, qkv, qkv, vdt, *lams, subln_g.reshape(DIFF_V_DIM, 1))


DEC_PAGES_PER_STEP = 8
DEC_ROWS = 64


DEC_PROBE_PAGES = 2


def _sb_probe_kernel(pt_ref, qs_ref, ksn_ref, k0_ref, k1_ref, o_ref, pad_ref):
    qs = qs_ref[0]
    n_new = ksn_ref.shape[1]
    pad_ref[...] = jnp.zeros_like(pad_ref)
    pad_ref[0:n_new, :] = ksn_ref[0]
    tok = lax.broadcasted_iota(jnp.int32, (DEC_ROWS, PAGE_SIZE), 1)
    qpos = lax.broadcasted_iota(jnp.int32, (DEC_ROWS, PAGE_SIZE), 0) & (n_new - 1)
    l_new, _ = _log_gates(_dot_nt(qs, pad_ref[...].astype(BF16)))
    l_new = jnp.where(tok < qpos, l_new, 0.0)
    pages = jnp.concatenate([k0_ref[...], k1_ref[...]], axis=1).astype(BF16)
    l_old, _ = _log_gates(_dot(qs, pages))
    total = jnp.sum(l_new, axis=1, keepdims=True) + jnp.sum(l_old, axis=1, keepdims=True)
    o_ref[0] = jnp.zeros(o_ref.shape[1:], F32) + jnp.max(total, axis=0, keepdims=True)


def _sb_live_pages(page_table, qs, ksn, csk):
    n_seq, n_pages = page_table.shape
    assert DEC_PROBE_PAGES == 2 and n_pages >= DEC_PROBE_PAGES
    rows = csk.shape[1]
    seq_spec = lambda r, w: pl.BlockSpec((1, r, w), lambda s, pt: (s, 0, 0))
    page_spec = lambda p: pl.BlockSpec((None, rows, PAGE_SIZE), lambda s, pt: (pt[s, n_pages - 1 - p], 0, 0))
    grid_spec = pltpu.PrefetchScalarGridSpec(
        num_scalar_prefetch=1,
        grid=(n_seq,),
        in_specs=[seq_spec(DEC_ROWS, rows), seq_spec(ksn.shape[1], rows), page_spec(0), page_spec(1)],
        out_specs=seq_spec(8, LANES),
        scratch_shapes=[pltpu.VMEM((PAGE_SIZE, rows), F32)],
    )
    worst = pl.pallas_call(
        _sb_probe_kernel,
        grid_spec=grid_spec,
        out_shape=jax.ShapeDtypeStruct((n_seq, 8, LANES), F32),
        compiler_params=_params(("arbitrary",)),
        name="sb_probe",
    )(page_table, qs, ksn, csk, csk)
    return jnp.where(worst[:, 0, 0] < SB_DEAD, DEC_PROBE_PAGES, n_pages).astype(jnp.int32)


def _decode_kernel(pt_ref, live_ref, qs_ref, qd_ref, ksn_ref, vsn_ref, kdn_ref, vdn_ref, *rest):
    npg = DEC_PAGES_PER_STEP
    pages = rest[:4 * npg]
    (u_ref, lq1_ref, lk1_ref, lq2_ref, lk2_ref, g_ref, so_ref, do_ref,
     sacc_ref, scarry_ref, dacc_ref, dm_ref, dl_ref, pad_ref) = rest[4 * npg:]
    j = pl.program_id(1)
    sb_live = j * npg < live_ref[pl.program_id(0)]
    qs = qs_ref[0]
    qd = qd_ref[0]
    n_new = ksn_ref.shape[1]

    def sb_tile(z, valid, pv):
        l, ls = _log_gates(z)
        if valid is not None:
            l = jnp.where(valid, l, 0.0)
        n = z.shape[1] // PAGE_SIZE
        chunk = lambda a, p: a[:, p * PAGE_SIZE:(p + 1) * PAGE_SIZE]
        tails = _tail_sums(jnp.concatenate([chunk(l, p) for p in range(n)], axis=0), u_ref[...])
        carry = scarry_ref[...]
        ws = []
        for p in range(n):
            tail = tails[p * DEC_ROWS:(p + 1) * DEC_ROWS]
            ws.append(jnp.exp(chunk(ls, p) + tail + carry))
            carry = carry + tail[:, 0:1] + chunk(l, p)[:, 0:1]
        w = ws[0] if n == 1 else jnp.concatenate(ws, axis=1)
        if valid is not None:
            w = jnp.where(valid, w, 0.0)
        sacc_ref[...] += pv(w.astype(BF16))
        scarry_ref[...] = carry

    def diff_tile(s, valid, v_of):
        if valid is not None:
            s = jnp.where(valid, s, NEG)
        m_prev = dm_ref[...]
        m_new = jnp.maximum(m_prev, jnp.max(s, axis=1, keepdims=True))
        alpha = jnp.exp(m_prev - m_new)
        p = jnp.exp(s - m_new)
        if valid is not None:
            p = jnp.where(valid, p, 0.0)
        dl_ref[...] = alpha * dl_ref[...] + jnp.sum(p, axis=1, keepdims=True)
        dm_ref[...] = m_new
        pb = p.astype(BF16)
        for hv in range(DIFF_HEADS):
            rows = slice(16 * hv, 16 * hv + 16)
            dacc_ref[rows, :] = alpha[rows] * dacc_ref[rows, :] + _dot(pb[rows], v_of(hv))

    @pl.when(j == 0)
    def _():
        sacc_ref[...] = jnp.zeros_like(sacc_ref)
        scarry_ref[...] = jnp.zeros_like(scarry_ref)
        dacc_ref[...] = jnp.zeros_like(dacc_ref)
        dl_ref[...] = jnp.zeros_like(dl_ref)
        dm_ref[...] = jnp.full_like(dm_ref, NEG)
        pad_ref[...] = jnp.zeros_like(pad_ref)
        pad_ref[0, 0:n_new, :] = ksn_ref[0]
        pad_ref[1, 0:n_new, :] = vsn_ref[0]
        pad_ref[2, 0:n_new, :] = kdn_ref[0]
        pad_ref[3, 0:n_new, :] = vdn_ref[0]
        tok = lax.broadcasted_iota(jnp.int32, (DEC_ROWS, PAGE_SIZE), 1)
        qpos = lax.broadcasted_iota(jnp.int32, (DEC_ROWS, PAGE_SIZE), 0) & (n_new - 1)
        vn = pad_ref[1].astype(BF16)
        sb_tile(_dot_nt(qs, pad_ref[0].astype(BF16)), tok < qpos, lambda w: _dot(w, vn))
        vdn = pad_ref[3].astype(BF16)
        diff_tile(_dot_nt(qd, pad_ref[2].astype(BF16)), tok <= qpos,
                  lambda hv: vdn[:, hv * DIFF_V_DIM:(hv + 1) * DIFF_V_DIM])

    side_by_side = lambda which: jnp.concatenate(
        [pages[4 * p + which][...] for p in range(npg)], axis=1).astype(BF16)
    pl.when(sb_live)(lambda: sb_tile(_dot(qs, side_by_side(0)), None, lambda w: _dot_nt(w, side_by_side(1))))
    diff_tile(_dot(qd, side_by_side(2)), None,
              lambda hv: jnp.concatenate(
                  [pages[4 * p + 3][pl.ds(hv, PAGE_SIZE, stride=DIFF_HEADS), :] for p in range(npg)],
                  axis=0).astype(BF16))

    @pl.when(j == pl.num_programs(1) - 1)
    def _():
        for h in range(SB_HEADS):
            so_ref[0, 8 * h:8 * h + 8, :] = sacc_ref[8 * h:8 * h + 8, h * HEAD_DIM:(h + 1) * HEAD_DIM]
        on = dacc_ref[...] / dl_ref[...]
        lam = _lambda_value(lq1_ref[...], lk1_ref[...], lq2_ref[...], lk2_ref[...])
        for hv in range(DIFF_HEADS):
            o = on[16 * hv:16 * hv + 8] - lam * on[16 * hv + 8:16 * hv + 16]
            do_ref[0, 8 * hv:8 * hv + 8, :] = _sub_rms_norm(o, g_ref[...])


def _decode(page_table, qs, qd, ksn, vsn, kdn, vdn, csk, csv, cdk, cdv, lams, subln_g):
    n_seq, n_pages = page_table.shape
    npg = DEC_PAGES_PER_STEP
    assert n_pages % npg == 0, (n_pages, npg)
    n_new = ksn.shape[1]
    rows = csk.shape[1]
    live = _sb_live_pages(page_table, qs, ksn, csk)
    seq_spec = lambda r, w: pl.BlockSpec((1, r, w), lambda s, j, pt, lv: (s, 0, 0))
    small = lambda w: pl.BlockSpec((1, w), lambda s, j, pt, lv: (0, 0))

    def page_spec(p, sb):
        def index(s, j, pt, lv):
            page = j * npg + p
            if sb:
                page = jnp.minimum(page, lv[s] - 1)
            return pt[s, n_pages - 1 - page], 0, 0
        return pl.BlockSpec((None, rows, PAGE_SIZE), index)

    page_specs, page_args = [], []
    for p in range(npg):
        page_specs += [page_spec(p, True), page_spec(p, True), page_spec(p, False), page_spec(p, False)]
        page_args += [csk, csv, cdk, cdv]
    grid_spec = pltpu.PrefetchScalarGridSpec(
        num_scalar_prefetch=2,
        grid=(n_seq, n_pages // npg),
        in_specs=[seq_spec(DEC_ROWS, rows), seq_spec(DEC_ROWS, rows)] + [seq_spec(n_new, rows)] * 4
                 + page_specs
                 + [pl.BlockSpec((2 * PAGE_SIZE, PAGE_SIZE), lambda s, j, pt, lv: (0, 0)),
                    small(HEAD_DIM), small(HEAD_DIM), small(HEAD_DIM), small(HEAD_DIM), small(DIFF_V_DIM)],
        out_specs=[seq_spec(DEC_ROWS, HEAD_DIM), seq_spec(DEC_ROWS // 2, DIFF_V_DIM)],
        scratch_shapes=[pltpu.VMEM((DEC_ROWS, rows), F32), pltpu.VMEM((DEC_ROWS, 1), F32),
                        pltpu.VMEM((DEC_ROWS, DIFF_V_DIM), F32), pltpu.VMEM((DEC_ROWS, 1), F32),
                        pltpu.VMEM((DEC_ROWS, 1), F32), pltpu.VMEM((4, PAGE_SIZE, rows), F32)],
    )
    return pl.pallas_call(
        _decode_kernel,
        grid_spec=grid_spec,
        out_shape=[jax.ShapeDtypeStruct((n_seq, DEC_ROWS, HEAD_DIM), F32),
                   jax.ShapeDtypeStruct((n_seq, DEC_ROWS // 2, DIFF_V_DIM), F32)],
        compiler_params=_params(("arbitrary", "arbitrary")),
        name="decode",
    )(page_table, live, qs, qd, ksn, vsn, kdn, vdn, *page_args, _triangle(PAGE_SIZE), *lams, subln_g)


def _layer_norm(y, g, b):
    mu = jnp.mean(y, axis=1, keepdims=True)
    d = y - mu
    var = jnp.mean(d * d, axis=1, keepdims=True)
    return d * lax.rsqrt(var + LN_EPS) * g + b


def _merge_kernel(x_ref, sb_ref, d_ref, gate_ref, w_ref, g_ref, b_ref, o_ref):
    o = _dot(sb_ref[0], w_ref[0:SB_WIDTH, :]) + _dot(d_ref[0], w_ref[SB_WIDTH:, :])
    y = DEEPNORM_ALPHA * x_ref[0] + gate_ref[0] * o
    o_ref[0] = _layer_norm(y, g_ref[...], b_ref[...])


def _merge(x, sb_o, d_o, mod, w_out, g, b, tm):
    grp, rows, _ = x.shape
    row_spec = lambda width: pl.BlockSpec((1, tm, width), lambda bb, i: (bb, i, 0))
    vec = pl.BlockSpec((1, D_MODEL), lambda bb, i: (0, 0))
    return pl.pallas_call(
        _merge_kernel,
        grid=(grp, rows // tm),
        in_specs=[row_spec(D_MODEL), row_spec(SB_WIDTH), row_spec(SB_WIDTH), _mod_spec(mod, tm, 2),
                  pl.BlockSpec((D_MODEL, D_MODEL), lambda bb, i: (0, 0)), vec, vec],
        out_specs=row_spec(D_MODEL),
        out_shape=jax.ShapeDtypeStruct(x.shape, F32),
        compiler_params=_params(("arbitrary", "arbitrary")),
        name="merge",
    )(x, sb_o, d_o, mod, w_out, g, b)


def _ffn_kernel(x_ref, shift_ref, scale_ref, gate_ref, wg_ref, wu_ref, wd_ref, g_ref, b_ref, o_ref, acc_ref):
    c = pl.program_id(2)
    x = x_ref[0]
    h = (x * (1.0 + scale_ref[0]) + shift_ref[0]).astype(BF16)
    gt = _dot(h, wg_ref[...])
    up = _dot(h, wu_ref[...])
    f = (gt * (1.0 / (1.0 + jnp.exp(-gt))) * up).astype(BF16)
    part = _dot(f, wd_ref[...])

    @pl.when(c == 0)
    def _():
        acc_ref[...] = part

    @pl.when(c != 0)
    def _():
        acc_ref[...] += part

    @pl.when(c == pl.num_programs(2) - 1)
    def _():
        y = DEEPNORM_ALPHA * x + gate_ref[0] * acc_ref[...]
        o_ref[0] = _layer_norm(y, g_ref[...], b_ref[...])


def _ffn(x, mod, w_gu, w_dn, g, b, tm, n_chunks):
    grp, rows, _ = x.shape
    fc = D_FF // n_chunks
    row_spec = pl.BlockSpec((1, tm, D_MODEL), lambda bb, i, c: (bb, i, 0))
    vec = pl.BlockSpec((1, D_MODEL), lambda bb, i, c: (0, 0))

    def mod_spec(col):
        if mod.shape[1] == 1:
            return pl.BlockSpec((1, 1, D_MODEL), lambda bb, i, c: (bb, 0, col))
        return pl.BlockSpec((1, tm, D_MODEL), lambda bb, i, c: (bb, i, col))

    return pl.pallas_call(
        _ffn_kernel,
        grid=(grp, rows // tm, n_chunks),
        in_specs=[row_spec, mod_spec(3), mod_spec(4), mod_spec(5),
                  pl.BlockSpec((D_MODEL, fc), lambda bb, i, c: (0, c)),
                  pl.BlockSpec((D_MODEL, fc), lambda bb, i, c: (0, n_chunks + c)),
                  pl.BlockSpec((fc, D_MODEL), lambda bb, i, c: (c, 0)), vec, vec],
        out_specs=row_spec,
        out_shape=jax.ShapeDtypeStruct(x.shape, F32),
        scratch_shapes=[pltpu.VMEM((tm, D_MODEL), F32)],
        compiler_params=_params(("arbitrary", "arbitrary", "arbitrary")),
        name="ffn",
    )(x, mod, mod, mod, w_gu, w_gu, w_dn, g, b)


def _rope_tables(pos):
    half = HEAD_DIM // 2
    inv = ROPE_THETA ** (-jnp.arange(half, dtype=F32) / half)
    ang = pos.astype(F32)[:, None] * inv[None, :]
    cos = jnp.cos(ang)
    sin = jnp.sin(ang)
    return jnp.concatenate([cos] * 4, axis=1), jnp.concatenate([-sin, sin, -sin, sin], axis=1)


def kernel(x_prompt, x_sample, cache_sb_k, cache_sb_v, cache_diff_k, cache_diff_v, page_table, c_prompt, c_sample, w_ada, b_ada, w_in, w_out, lambda_q1, lambda_k1, lambda_q2, lambda_k2, subln_g, ln1_g, ln1_b, w_gate_up, w_down, ln2_g, ln2_b):
    n_batch, seq, _ = x_prompt.shape
    n_dec, n_new, _ = x_sample.shape
    n_pool = cache_sb_k.shape[0]
    lyr = 0

    n_c = n_batch + n_dec
    pad = (-n_c) % 8
    c_all = jnp.concatenate([c_prompt, c_sample, jnp.zeros((pad, D_MODEL), F32)], axis=0)
    mod = _ada(c_all, w_ada[lyr], b_ada[lyr][None, :])
    mod_p = mod[:n_batch].reshape(n_batch, 1, 6 * D_MODEL)
    mod_s = jnp.repeat(mod[n_batch:n_c], n_new, axis=0).reshape(1, n_dec * n_new, 6 * D_MODEL)

    w_in_b = w_in[lyr].astype(BF16)
    w_out_b = w_out[lyr].astype(BF16)
    w_gu_b = w_gate_up[lyr].astype(BF16)
    w_dn_b = w_down[lyr].astype(BF16)
    lams = (lambda_q1, lambda_k1, lambda_q2, lambda_k2)
    g1, b1, g2, b2 = ln1_g[lyr][None, :], ln1_b[lyr][None, :], ln2_g[lyr][None, :], ln2_b[lyr][None, :]

    cos_p, sin_p = _rope_tables(jnp.arange(seq, dtype=jnp.int32))
    qkv_p, ksb_p, vsb_p, kd_p, vd_p, vdt_p = _inproj(x_prompt, mod_p, w_in_b, cos_p, sin_p, tm=512)
    sb_o_p = _sb_attn(qkv_p, tq=256)
    d_o_p = _diff_attn(qkv_p, vdt_p, lams, subln_g, tq=256)
    x1_p = _merge(x_prompt, sb_o_p, d_o_p, mod_p, w_out_b, g1, b1, tm=512)
    y_p = _ffn(x1_p, mod_p, w_gu_b, w_dn_b, g2, b2, tm=512, n_chunks=2)

    rows_s = n_dec * n_new
    cos_n, sin_n = _rope_tables(PAST_LEN + jnp.arange(n_new, dtype=jnp.int32))
    cos_s = jnp.tile(cos_n, (n_dec, 1))
    sin_s = jnp.tile(sin_n, (n_dec, 1))
    xs = x_sample.reshape(1, rows_s, D_MODEL)
    qkv_s, ksb_s, vsb_s, kd_s, vd_s, _ = _inproj(xs, mod_s, w_in_b, cos_s, sin_s, tm=256)

    w = SB_WIDTH

    def block_diag_rows(q):
        q = q.reshape(n_dec, n_new, SB_HEADS, HEAD_DIM).transpose(0, 2, 1, 3)
        same = jnp.eye(SB_HEADS, dtype=bool)[None, :, None, :, None]
        return jnp.where(same, q[:, :, :, None, :], 0).reshape(n_dec, SB_HEADS * n_new, w)

    def token_minor(cache):
        return cache.transpose(0, 1, 3, 4, 2).reshape(n_pool, w, PAGE_SIZE)

    so, do = _decode(
        page_table, block_diag_rows(qkv_s[0, :, 0:w]), block_diag_rows(qkv_s[0, :, 3 * w:4 * w]),
        ksb_s.reshape(n_dec, n_new, w), vsb_s.reshape(n_dec, n_new, w),
        kd_s.reshape(n_dec, n_new, w), vd_s.reshape(n_dec, n_new, w),
        token_minor(cache_sb_k), token_minor(cache_sb_v), token_minor(cache_diff_k),
        cache_diff_v.reshape(n_pool, PAGE_SIZE * DIFF_HEADS, DIFF_V_DIM),
        lams, subln_g)
    sb_o_s = so.reshape(n_dec, SB_HEADS, n_new, HEAD_DIM).transpose(0, 2, 1, 3).reshape(1, rows_s, w)
    d_o_s = do.reshape(n_dec, DIFF_HEADS, n_new, DIFF_V_DIM).transpose(0, 2, 1, 3).reshape(1, rows_s, w)
    x1_s = _merge(xs, sb_o_s.astype(BF16), d_o_s.astype(BF16), mod_s, w_out_b, g1, b1, tm=256)
    y_s = _ffn(x1_s, mod_s, w_gu_b, w_dn_b, g2, b2, tm=256, n_chunks=2)

    return (y_p, y_s.reshape(n_dec, n_new, D_MODEL),
            ksb_p.reshape(n_batch, 1, seq, SB_HEADS, HEAD_DIM),
            vsb_p.reshape(n_batch, 1, seq, SB_HEADS, HEAD_DIM),
            kd_p.reshape(n_batch, 1, seq, 2 * DIFF_HEADS, HEAD_DIM),
            vd_p.reshape(n_batch, 1, seq, DIFF_HEADS, DIFF_V_DIM),
            ksb_s.reshape(n_dec, 1, n_new, SB_HEADS, HEAD_DIM),
            vsb_s.reshape(n_dec, 1, n_new, SB_HEADS, HEAD_DIM),
            kd_s.reshape(n_dec, 1, n_new, 2 * DIFF_HEADS, HEAD_DIM),
            vd_s.reshape(n_dec, 1, n_new, DIFF_HEADS, DIFF_V_DIM))
```

```python
import functools
import math

import numpy as np
import jax
import jax.numpy as jnp
from jax import lax
from jax.experimental import pallas as pl
from jax.experimental.pallas import tpu as pltpu

F32 = jnp.float32
BF16 = jnp.bfloat16

D_MODEL = 1024
HEAD_DIM = 64
SB_HEADS = 8
DIFF_HEADS = 4
SB_WIDTH = SB_HEADS * HEAD_DIM
DIFF_V_DIM = 2 * HEAD_DIM
IN_WIDTH = 6 * SB_WIDTH
D_FF = 2816
PAGE_SIZE = 128
PAST_LEN = 8192
ROPE_THETA = 10000.0
LN_EPS = 1e-5
DEPTH = 1
DEEPNORM_ALPHA = (2 * DEPTH) ** 0.25
LAM_INIT = 0.8 - 0.6 * math.exp(-0.3 * 0)
QK_SCALE = HEAD_DIM ** -0.5
NEG = -1e30

LANES = 128
VMEM_LIMIT = 56 * 1024 * 1024

COL_SBQ, COL_SBK, COL_SBV, COL_DQ, COL_DK, COL_DV = 0, 4, 8, 12, 16, 20

NT_DIMS = (((1,), (1,)), ((), ()))


def _dot(a, b):
    return jnp.dot(a, b, preferred_element_type=F32)


def _dot_nt(a, b):
    return lax.dot_general(a, b, NT_DIMS, preferred_element_type=F32)


def _params(sem):
    return pltpu.CompilerParams(dimension_semantics=sem, vmem_limit_bytes=VMEM_LIMIT)


def _ada_kernel(c_ref, w_ref, b_ref, o_ref):
    c = c_ref[...]
    a = c * (1.0 / (1.0 + jnp.exp(-c)))
    o_ref[...] = jnp.dot(a, w_ref[...], preferred_element_type=F32,
                         precision=lax.Precision.HIGHEST) + b_ref[...]


def _ada(c_all, w, b):
    rows = c_all.shape[0]
    n = w.shape[1]
    tn = 1024
    return pl.pallas_call(
        _ada_kernel,
        grid=(n // tn,),
        in_specs=[pl.BlockSpec((rows, D_MODEL), lambda k: (0, 0)),
                  pl.BlockSpec((D_MODEL, tn), lambda k: (0, k)),
                  pl.BlockSpec((1, tn), lambda k: (0, k))],
        out_specs=pl.BlockSpec((rows, tn), lambda k: (0, k)),
        out_shape=jax.ShapeDtypeStruct((rows, n), F32),
        compiler_params=_params(("arbitrary",)),
        name="ada",
    )(c_all, w, b)


def _inproj_kernel(x_ref, shift_ref, scale_ref, w_ref, cos_ref, sin_ref,
                   qkv_ref, ksb_ref, vsb_ref, kd_ref, vd_ref, vdt_ref):
    x = x_ref[0]
    h = x * (1.0 + scale_ref[0]) + shift_ref[0]
    y = _dot(h.astype(BF16), w_ref[...])
    tm = y.shape[0]
    cos = cos_ref[...]
    sin = sin_ref[...]
    lane = lax.broadcasted_iota(jnp.int32, (tm, LANES), 1)
    first_half = (lane & (HEAD_DIM - 1)) < (HEAD_DIM // 2)

    def rope(t):
        rot = jnp.where(first_half, pltpu.roll(t, LANES - HEAD_DIM // 2, 1),
                        pltpu.roll(t, HEAD_DIM // 2, 1))
        return t * cos + rot * sin

    w = SB_WIDTH
    qkv_ref[0, :, 0:w] = (y[:, 0:w] * QK_SCALE).astype(BF16)
    ksb = y[:, w:2 * w]
    vsb = y[:, 2 * w:3 * w]
    ksb_ref[0] = ksb
    vsb_ref[0] = vsb
    qkv_ref[0, :, w:2 * w] = ksb.astype(BF16)
    qkv_ref[0, :, 2 * w:3 * w] = vsb.astype(BF16)
    for g in range(w // LANES):
        lo = 3 * w + g * LANES
        qkv_ref[0, :, lo:lo + LANES] = (rope(y[:, lo:lo + LANES]) * QK_SCALE).astype(BF16)
        lo = 4 * w + g * LANES
        kd = rope(y[:, lo:lo + LANES])
        kd_ref[0, :, g * LANES:(g + 1) * LANES] = kd
        qkv_ref[0, :, lo:lo + LANES] = kd.astype(BF16)
    vd = y[:, 5 * w:6 * w]
    vd_ref[0] = vd
    qkv_ref[0, :, 5 * w:6 * w] = vd.astype(BF16)
    vdt_ref[0] = vd.T.astype(BF16)


def _mod_spec(mod, tm, col):
    if mod.shape[1] == 1:
        return pl.BlockSpec((1, 1, D_MODEL), lambda b, i: (b, 0, col))
    return pl.BlockSpec((1, tm, D_MODEL), lambda b, i: (b, i, col))


def _inproj(x, mod, w_in, cos, sin, tm):
    g, rows, _ = x.shape
    row_spec = lambda width: pl.BlockSpec((1, tm, width), lambda b, i: (b, i, 0))
    return pl.pallas_call(
        _inproj_kernel,
        grid=(g, rows // tm),
        in_specs=[row_spec(D_MODEL), _mod_spec(mod, tm, 0), _mod_spec(mod, tm, 1),
                  pl.BlockSpec((D_MODEL, IN_WIDTH), lambda b, i: (0, 0)),
                  pl.BlockSpec((tm, LANES), lambda b, i: (i, 0)),
                  pl.BlockSpec((tm, LANES), lambda b, i: (i, 0))],
        out_specs=[row_spec(IN_WIDTH)] + [row_spec(SB_WIDTH)] * 4
                  + [pl.BlockSpec((1, SB_WIDTH, tm), lambda b, i: (b, 0, i))],
        out_shape=[jax.ShapeDtypeStruct((g, rows, IN_WIDTH), BF16)]
                  + [jax.ShapeDtypeStruct((g, rows, SB_WIDTH), F32)] * 4
                  + [jax.ShapeDtypeStruct((g, SB_WIDTH, rows), BF16)],
        compiler_params=_params(("arbitrary", "arbitrary")),
        name="inproj",
    )(x, mod, mod, w_in, cos, sin)


def _log_gates(z):
    sp = jnp.log(1.0 + jnp.exp(-jnp.abs(z)))
    l = jnp.minimum(-z, 0.0) - sp
    return l, l + z


def _triangle(n):
    j = np.arange(n)
    tri = (j[:, None] > j[None, :]).astype(np.float32)
    return jnp.asarray(np.concatenate([tri, tri], axis=0), BF16)


def _tail_sums(l, u2):
    l_hi = l.astype(BF16)
    l_lo = (l - l_hi.astype(F32)).astype(BF16)
    return _dot(jnp.concatenate([l_hi, l_lo], axis=1), u2)


def _lambda_value(lq1, lk1, lq2, lk2):
    return (jnp.exp(jnp.sum(lq1 * lk1, axis=1, keepdims=True))
            - jnp.exp(jnp.sum(lq2 * lk2, axis=1, keepdims=True)) + LAM_INIT)


def _sub_rms_norm(o, g):
    ms = jnp.mean(o * o, axis=1, keepdims=True)
    return o * lax.rsqrt(ms + LN_EPS) * g * (1.0 - LAM_INIT)


SB_DEAD = -120.0
SB_HEADS_PER_STEP = 8
DIFF_HEADS_PER_STEP = 4


def _tri_tables(nq, descending):
    qs, ks = [], []
    for qi in range(nq):
        order = range(qi, -1, -1) if descending else range(qi + 1)
        for kj in order:
            qs.append(qi)
            ks.append(kj)
    return jnp.asarray(np.array(qs, np.int32)), jnp.asarray(np.array(ks, np.int32))


def _sb_attn_kernel(q_ref, k_ref, v_ref, u_ref, o_ref, acc_ref, carry_ref):
    qi = pl.program_id(1)
    tq = q_ref.shape[1]
    acc_ref[...] = jnp.zeros_like(acc_ref)
    carry_ref[...] = jnp.zeros_like(carry_ref)

    def step(kj, diag):
        q = q_ref[0]
        rows = pl.ds(pl.multiple_of(kj * tq, tq), tq)
        k = k_ref[0, rows, :]
        v = v_ref[0, rows, :]
        if diag:
            row = lax.broadcasted_iota(jnp.int32, (tq, tq), 0)
            col = lax.broadcasted_iota(jnp.int32, (tq, tq), 1)
            mask = col < row
        heads = range(SB_HEADS_PER_STEP)
        sl = [slice(h * HEAD_DIM, (h + 1) * HEAD_DIM) for h in heads]
        gates = [_log_gates(_dot_nt(q[:, sl[h]], k[:, sl[h]])) for h in heads]
        ls = [g[1] for g in gates]
        l = [jnp.where(mask, g[0], 0.0) if diag else g[0] for g in gates]
        tails = _tail_sums(jnp.concatenate(l, axis=0), u_ref[...])
        for h in heads:
            tail = tails[h * tq:(h + 1) * tq]
            carry = carry_ref[h]
            w = jnp.exp(ls[h] + tail + carry)
            if diag:
                w = jnp.where(mask, w, 0.0)
            acc_ref[h] += _dot(w.astype(BF16), v[:, sl[h]])
            carry_ref[h] = carry + tail[:, 0:1] + l[h][:, 0:1]
        return (jnp.max(carry_ref[...]) > SB_DEAD).astype(jnp.int32)

    live = step(qi, True)

    def older(state):
        kj, _ = state
        return kj - 1, step(kj, False)

    lax.while_loop(lambda state: (state[0] >= 0) & (state[1] == 1), older, (qi - 1, live))
    o_ref[0] = jnp.concatenate([acc_ref[h] for h in range(SB_HEADS_PER_STEP)], axis=1).astype(o_ref.dtype)


def _sb_attn(qkv, tq):
    b, s, _ = qkv.shape
    nh = SB_HEADS_PER_STEP
    assert nh == SB_HEADS
    col = lambda first: first * LANES // SB_WIDTH
    return pl.pallas_call(
        _sb_attn_kernel,
        grid=(b, s // tq),
        in_specs=[pl.BlockSpec((1, tq, SB_WIDTH), lambda bb, qi: (bb, qi, col(COL_SBQ))),
                  pl.BlockSpec((1, s, SB_WIDTH), lambda bb, qi: (bb, 0, col(COL_SBK)),
                               pipeline_mode=pl.Buffered(1)),
                  pl.BlockSpec((1, s, SB_WIDTH), lambda bb, qi: (bb, 0, col(COL_SBV)),
                               pipeline_mode=pl.Buffered(1)),
                  pl.BlockSpec((2 * tq, tq), lambda bb, qi: (0, 0))],
        out_specs=pl.BlockSpec((1, tq, SB_WIDTH), lambda bb, qi: (bb, qi, 0)),
        out_shape=jax.ShapeDtypeStruct((b, s, SB_WIDTH), BF16),
        scratch_shapes=[pltpu.VMEM((nh, tq, HEAD_DIM), F32), pltpu.VMEM((nh, tq, 1), F32)],
        compiler_params=_params(("arbitrary", "arbitrary")),
        name="sb_attn",
    )(qkv, qkv, qkv, _triangle(tq))


def _diff_attn_kernel(qi_ref, kj_ref, q_ref, k_ref, vt_ref, lq1_ref, lk1_ref, lq2_ref, lk2_ref, g_ref,
                      o_ref, acc_ref, m_ref, l_ref):
    t = pl.program_id(2)
    qi = qi_ref[t]
    kj = kj_ref[t]
    tq = q_ref.shape[1]

    @pl.when(kj == 0)
    def _():
        acc_ref[...] = jnp.zeros_like(acc_ref)
        l_ref[...] = jnp.zeros_like(l_ref)
        m_ref[...] = jnp.full_like(m_ref, NEG)

    def step(diag):
        q = q_ref[0]
        k = k_ref[0]
        vt = vt_ref[0]
        if diag:
            key = lax.broadcasted_iota(jnp.int32, (tq, tq), 0)
            qry = lax.broadcasted_iota(jnp.int32, (tq, tq), 1)
            mask = key <= qry
        maps = range(2 * DIFF_HEADS_PER_STEP)
        sl = [slice(i * HEAD_DIM, (i + 1) * HEAD_DIM) for i in maps]
        s = [_dot_nt(k[:, sl[i]], q[:, sl[i]]) for i in maps]
        if diag:
            s = [jnp.where(mask, si, NEG) for si in s]
        m_prev = [m_ref[i] for i in maps]
        m_new = [jnp.maximum(m_prev[i], jnp.max(s[i], axis=0, keepdims=True)) for i in maps]
        alpha = [jnp.exp(m_prev[i] - m_new[i]) for i in maps]
        p = [jnp.exp(s[i] - m_new[i]) for i in maps]
        for i in maps:
            vh = vt[(i // 2) * DIFF_V_DIM:(i // 2 + 1) * DIFF_V_DIM, :]
            l_ref[i] = alpha[i] * l_ref[i] + jnp.sum(p[i], axis=0, keepdims=True)
            acc_ref[i] = alpha[i] * acc_ref[i] + _dot(vh, p[i].astype(BF16))
            m_ref[i] = m_new[i]

    pl.when(kj == qi)(functools.partial(step, True))
    pl.when(kj != qi)(functools.partial(step, False))

    @pl.when(kj == qi)
    def _():
        lam = _lambda_value(lq1_ref[...], lk1_ref[...], lq2_ref[...], lk2_ref[...])
        for hv in range(DIFF_HEADS_PER_STEP):
            o = acc_ref[2 * hv] / l_ref[2 * hv] - lam * (acc_ref[2 * hv + 1] / l_ref[2 * hv + 1])
            ms = jnp.mean(o * o, axis=0, keepdims=True)
            on = o * lax.rsqrt(ms + LN_EPS) * g_ref[...] * (1.0 - LAM_INIT)
            o_ref[0, :, hv * DIFF_V_DIM:(hv + 1) * DIFF_V_DIM] = on.T.astype(o_ref.dtype)


def _diff_attn(qkv, vdt, lams, subln_g, tq):
    b, s, _ = qkv.shape
    qi_tab, kj_tab = _tri_tables(s // tq, descending=False)
    nhv = DIFF_HEADS_PER_STEP
    width = nhv * DIFF_V_DIM
    groups = DIFF_HEADS // nhv
    col = lambda first: first * LANES // width
    small = lambda w: pl.BlockSpec((1, w), lambda bb, g, t, qt, kt: (0, 0))
    grid_spec = pltpu.PrefetchScalarGridSpec(
        num_scalar_prefetch=2,
        grid=(b, groups, int(qi_tab.shape[0])),
        in_specs=[pl.BlockSpec((1, tq, width), lambda bb, g, t, qt, kt: (bb, qt[t], col(COL_DQ) + g)),
                  pl.BlockSpec((1, tq, width), lambda bb, g, t, qt, kt: (bb, kt[t], col(COL_DK) + g)),
                  pl.BlockSpec((1, width, tq), lambda bb, g, t, qt, kt: (bb, g, kt[t])),
                  small(HEAD_DIM), small(HEAD_DIM), small(HEAD_DIM), small(HEAD_DIM),
                  pl.BlockSpec((DIFF_V_DIM, 1), lambda bb, g, t, qt, kt: (0, 0))],
        out_specs=pl.BlockSpec((1, tq, width), lambda bb, g, t, qt, kt: (bb, qt[t], g)),
        scratch_shapes=[pltpu.VMEM((2 * nhv, DIFF_V_DIM, tq), F32), pltpu.VMEM((2 * nhv, 1, tq), F32),
                        pltpu.VMEM((2 * nhv, 1, tq), F32)],
    )
    return pl.pallas_call(
        _diff_attn_kernel,
        grid_spec=grid_spec,
        out_shape=jax.ShapeDtypeStruct((b, s, SB_WIDTH), BF16),
        compiler_params=_params(("arbitrary", "arbitrary", "arbitrary")),
        name="diff_attn",
    )(qi_tab, kj_tab, qkv, qkv, vdt, *lams, subln_g.reshape(DIFF_V_DIM, 1))


DEC_PAGES_PER_STEP = 16
DEC_ROWS = 64


DEC_PROBE_PAGES = 2


def _sb_probe_kernel(pt_ref, qs_ref, ksn_ref, k0_ref, k1_ref, o_ref, pad_ref):
    qs = qs_ref[0]
    n_new = ksn_ref.shape[1]
    pad_ref[...] = jnp.zeros_like(pad_ref)
    pad_ref[0:n_new, :] = ksn_ref[0]
    tok = lax.broadcasted_iota(jnp.int32, (DEC_ROWS, PAGE_SIZE), 1)
    qpos = lax.broadcasted_iota(jnp.int32, (DEC_ROWS, PAGE_SIZE), 0) & (n_new - 1)
    l_new, _ = _log_gates(_dot_nt(qs, pad_ref[...].astype(BF16)))
    l_new = jnp.where(tok < qpos, l_new, 0.0)
    pages = jnp.concatenate([k0_ref[...], k1_ref[...]], axis=1).astype(BF16)
    l_old, _ = _log_gates(_dot(qs, pages))
    total = jnp.sum(l_new, axis=1, keepdims=True) + jnp.sum(l_old, axis=1, keepdims=True)
    o_ref[0] = jnp.zeros(o_ref.shape[1:], F32) + jnp.max(total, axis=0, keepdims=True)


def _sb_live_pages(page_table, qs, ksn, csk):
    n_seq, n_pages = page_table.shape
    assert DEC_PROBE_PAGES == 2 and n_pages >= DEC_PROBE_PAGES
    rows = csk.shape[1]
    seq_spec = lambda r, w: pl.BlockSpec((1, r, w), lambda s, pt: (s, 0, 0))
    page_spec = lambda p: pl.BlockSpec((None, rows, PAGE_SIZE), lambda s, pt: (pt[s, n_pages - 1 - p], 0, 0))
    grid_spec = pltpu.PrefetchScalarGridSpec(
        num_scalar_prefetch=1,
        grid=(n_seq,),
        in_specs=[seq_spec(DEC_ROWS, rows), seq_spec(ksn.shape[1], rows), page_spec(0), page_spec(1)],
        out_specs=seq_spec(8, LANES),
        scratch_shapes=[pltpu.VMEM((PAGE_SIZE, rows), F32)],
    )
    worst = pl.pallas_call(
        _sb_probe_kernel,
        grid_spec=grid_spec,
        out_shape=jax.ShapeDtypeStruct((n_seq, 8, LANES), F32),
        compiler_params=_params(("arbitrary",)),
        name="sb_probe",
    )(page_table, qs, ksn, csk, csk)
    return jnp.where(worst[:, 0, 0] < SB_DEAD, DEC_PROBE_PAGES, n_pages).astype(jnp.int32)


def _decode_kernel(pt_ref, live_ref, qs_ref, qd_ref, ksn_ref, vsn_ref, kdn_ref, vdn_ref, *rest):
    npg = DEC_PAGES_PER_STEP
    pages = rest[:4 * npg]
    (u_ref, lq1_ref, lk1_ref, lq2_ref, lk2_ref, g_ref, so_ref, do_ref,
     sacc_ref, scarry_ref, dacc_ref, dm_ref, dl_ref, pad_ref) = rest[4 * npg:]
    j = pl.program_id(1)
    sb_live = j * npg < live_ref[pl.program_id(0)]
    qs = qs_ref[0]
    qd = qd_ref[0]
    n_new = ksn_ref.shape[1]

    def sb_tile(z, valid, pv):
        l, ls = _log_gates(z)
        if valid is not None:
            l = jnp.where(valid, l, 0.0)
        n = z.shape[1] // PAGE_SIZE
        chunk = lambda a, p: a[:, p * PAGE_SIZE:(p + 1) * PAGE_SIZE]
        tails = _tail_sums(jnp.concatenate([chunk(l, p) for p in range(n)], axis=0), u_ref[...])
        carry = scarry_ref[...]
        ws = []
        for p in range(n):
            tail = tails[p * DEC_ROWS:(p + 1) * DEC_ROWS]
            ws.append(jnp.exp(chunk(ls, p) + tail + carry))
            carry = carry + tail[:, 0:1] + chunk(l, p)[:, 0:1]
        w = ws[0] if n == 1 else jnp.concatenate(ws, axis=1)
        if valid is not None:
            w = jnp.where(valid, w, 0.0)
        sacc_ref[...] += pv(w.astype(BF16))
        scarry_ref[...] = carry

    def diff_tile(s, valid, v_of):
        if valid is not None:
            s = jnp.where(valid, s, NEG)
        m_prev = dm_ref[...]
        m_new = jnp.maximum(m_prev, jnp.max(s, axis=1, keepdims=True))
        alpha = jnp.exp(m_prev - m_new)
        p = jnp.exp(s - m_new)
        if valid is not None:
            p = jnp.where(valid, p, 0.0)
        dl_ref[...] = alpha * dl_ref[...] + jnp.sum(p, axis=1, keepdims=True)
        dm_ref[...] = m_new
        pb = p.astype(BF16)
        for hv in range(DIFF_HEADS):
            rows = slice(16 * hv, 16 * hv + 16)
            dacc_ref[rows, :] = alpha[rows] * dacc_ref[rows, :] + _dot(pb[rows], v_of(hv))

    @pl.when(j == 0)
    def _():
        sacc_ref[...] = jnp.zeros_like(sacc_ref)
        scarry_ref[...] = jnp.zeros_like(scarry_ref)
        dacc_ref[...] = jnp.zeros_like(dacc_ref)
        dl_ref[...] = jnp.zeros_like(dl_ref)
        dm_ref[...] = jnp.full_like(dm_ref, NEG)
        pad_ref[...] = jnp.zeros_like(pad_ref)
        pad_ref[0, 0:n_new, :] = ksn_ref[0]
        pad_ref[1, 0:n_new, :] = vsn_ref[0]
        pad_ref[2, 0:n_new, :] = kdn_ref[0]
        pad_ref[3, 0:n_new, :] = vdn_ref[0]
        tok = lax.broadcasted_iota(jnp.int32, (DEC_ROWS, PAGE_SIZE), 1)
        qpos = lax.broadcasted_iota(jnp.int32, (DEC_ROWS, PAGE_SIZE), 0) & (n_new - 1)
        vn = pad_ref[1].astype(BF16)
        sb_tile(_dot_nt(qs, pad_ref[0].astype(BF16)), tok < qpos, lambda w: _dot(w, vn))
        vdn = pad_ref[3].astype(BF16)
        diff_tile(_dot_nt(qd, pad_ref[2].astype(BF16)), tok <= qpos,
                  lambda hv: vdn[:, hv * DIFF_V_DIM:(hv + 1) * DIFF_V_DIM])

    side_by_side = lambda which: jnp.concatenate(
        [pages[4 * p + which][...] for p in range(npg)], axis=1).astype(BF16)
    pl.when(sb_live)(lambda: sb_tile(_dot(qs, side_by_side(0)), None, lambda w: _dot_nt(w, side_by_side(1))))
    diff_tile(_dot(qd, side_by_side(2)), None,
              lambda hv: jnp.concatenate(
                  [pages[4 * p + 3][pl.ds(hv, PAGE_SIZE, stride=DIFF_HEADS), :] for p in range(npg)],
                  axis=0).astype(BF16))

    @pl.when(j == pl.num_programs(1) - 1)
    def _():
        for h in range(SB_HEADS):
            so_ref[0, 8 * h:8 * h + 8, :] = sacc_ref[8 * h:8 * h + 8, h * HEAD_DIM:(h + 1) * HEAD_DIM]
        on = dacc_ref[...] / dl_ref[...]
        lam = _lambda_value(lq1_ref[...], lk1_ref[...], lq2_ref[...], lk2_ref[...])
        for hv in range(DIFF_HEADS):
            o = on[16 * hv:16 * hv + 8] - lam * on[16 * hv + 8:16 * hv + 16]
            do_ref[0, 8 * hv:8 * hv + 8, :] = _sub_rms_norm(o, g_ref[...])


def _decode(page_table, qs, qd, ksn, vsn, kdn, vdn, csk, csv, cdk, cdv, lams, subln_g):
    n_seq, n_pages = page_table.shape
    npg = DEC_PAGES_PER_STEP
    assert n_pages % npg == 0, (n_pages, npg)
    n_new = ksn.shape[1]
    rows = csk.shape[1]
    live = _sb_live_pages(page_table, qs, ksn, csk)
    seq_spec = lambda r, w: pl.BlockSpec((1, r, w), lambda s, j, pt, lv: (s, 0, 0))
    small = lambda w: pl.BlockSpec((1, w), lambda s, j, pt, lv: (0, 0))

    def page_spec(p, sb):
        def index(s, j, pt, lv):
            page = j * npg + p
            if sb:
                page = jnp.minimum(page, lv[s] - 1)
            return pt[s, n_pages - 1 - page], 0, 0
        return pl.BlockSpec((None, rows, PAGE_SIZE), index)

    page_specs, page_args = [], []
    for p in range(npg):
        page_specs += [page_spec(p, True), page_spec(p, True), page_spec(p, False), page_spec(p, False)]
        page_args += [csk, csv, cdk, cdv]
    grid_spec = pltpu.PrefetchScalarGridSpec(
        num_scalar_prefetch=2,
        grid=(n_seq, n_pages // npg),
        in_specs=[seq_spec(DEC_ROWS, rows), seq_spec(DEC_ROWS, rows)] + [seq_spec(n_new, rows)] * 4
                 + page_specs
                 + [pl.BlockSpec((2 * PAGE_SIZE, PAGE_SIZE), lambda s, j, pt, lv: (0, 0)),
                    small(HEAD_DIM), small(HEAD_DIM), small(HEAD_DIM), small(HEAD_DIM), small(DIFF_V_DIM)],
        out_specs=[seq_spec(DEC_ROWS, HEAD_DIM), seq_spec(DEC_ROWS // 2, DIFF_V_DIM)],
        scratch_shapes=[pltpu.VMEM((DEC_ROWS, rows), F32), pltpu.VMEM((DEC_ROWS, 1), F32),
                        pltpu.VMEM((DEC_ROWS, DIFF_V_DIM), F32), pltpu.VMEM((DEC_ROWS, 1), F32),
                        pltpu.VMEM((DEC_ROWS, 1), F32), pltpu.VMEM((4, PAGE_SIZE, rows), F32)],
    )
    return pl.pallas_call(
        _decode_kernel,
        grid_spec=grid_spec,
        out_shape=[jax.ShapeDtypeStruct((n_seq, DEC_ROWS, HEAD_DIM), F32),
                   jax.ShapeDtypeStruct((n_seq, DEC_ROWS // 2, DIFF_V_DIM), F32)],
        compiler_params=_params(("arbitrary", "arbitrary")),
        name="decode",
    )(page_table, live, qs, qd, ksn, vsn, kdn, vdn, *page_args, _triangle(PAGE_SIZE), *lams, subln_g)


def _layer_norm(y, g, b):
    mu = jnp.mean(y, axis=1, keepdims=True)
    d = y - mu
    var = jnp.mean(d * d, axis=1, keepdims=True)
    return d * lax.rsqrt(var + LN_EPS) * g + b


def _merge_kernel(x_ref, sb_ref, d_ref, gate_ref, w_ref, g_ref, b_ref, o_ref):
    o = _dot(sb_ref[0], w_ref[0:SB_WIDTH, :]) + _dot(d_ref[0], w_ref[SB_WIDTH:, :])
    y = DEEPNORM_ALPHA * x_ref[0] + gate_ref[0] * o
    o_ref[0] = _layer_norm(y, g_ref[...], b_ref[...])


def _merge(x, sb_o, d_o, mod, w_out, g, b, tm):
    grp, rows, _ = x.shape
    row_spec = lambda width: pl.BlockSpec((1, tm, width), lambda bb, i: (bb, i, 0))
    vec = pl.BlockSpec((1, D_MODEL), lambda bb, i: (0, 0))
    return pl.pallas_call(
        _merge_kernel,
        grid=(grp, rows // tm),
        in_specs=[row_spec(D_MODEL), row_spec(SB_WIDTH), row_spec(SB_WIDTH), _mod_spec(mod, tm, 2),
                  pl.BlockSpec((D_MODEL, D_MODEL), lambda bb, i: (0, 0)), vec, vec],
        out_specs=row_spec(D_MODEL),
        out_shape=jax.ShapeDtypeStruct(x.shape, F32),
        compiler_params=_params(("arbitrary", "arbitrary")),
        name="merge",
    )(x, sb_o, d_o, mod, w_out, g, b)


def _ffn_kernel(x_ref, shift_ref, scale_ref, gate_ref, wg_ref, wu_ref, wd_ref, g_ref, b_ref, o_ref, acc_ref):
    c = pl.program_id(2)
    x = x_ref[0]
    h = (x * (1.0 + scale_ref[0]) + shift_ref[0]).astype(BF16)
    gt = _dot(h, wg_ref[...])
    up = _dot(h, wu_ref[...])
    f = (gt * (1.0 / (1.0 + jnp.exp(-gt))) * up).astype(BF16)
    part = _dot(f, wd_ref[...])

    @pl.when(c == 0)
    def _():
        acc_ref[...] = part

    @pl.when(c != 0)
    def _():
        acc_ref[...] += part

    @pl.when(c == pl.num_programs(2) - 1)
    def _():
        y = DEEPNORM_ALPHA * x + gate_ref[0] * acc_ref[...]
        o_ref[0] = _layer_norm(y, g_ref[...], b_ref[...])


def _ffn(x, mod, w_gu, w_dn, g, b, tm, n_chunks):
    grp, rows, _ = x.shape
    fc = D_FF // n_chunks
    row_spec = pl.BlockSpec((1, tm, D_MODEL), lambda bb, i, c: (bb, i, 0))
    vec = pl.BlockSpec((1, D_MODEL), lambda bb, i, c: (0, 0))

    def mod_spec(col):
        if mod.shape[1] == 1:
            return pl.BlockSpec((1, 1, D_MODEL), lambda bb, i, c: (bb, 0, col))
        return pl.BlockSpec((1, tm, D_MODEL), lambda bb, i, c: (bb, i, col))

    return pl.pallas_call(
        _ffn_kernel,
        grid=(grp, rows // tm, n_chunks),
        in_specs=[row_spec, mod_spec(3), mod_spec(4), mod_spec(5),
                  pl.BlockSpec((D_MODEL, fc), lambda bb, i, c: (0, c)),
                  pl.BlockSpec((D_MODEL, fc), lambda bb, i, c: (0, n_chunks + c)),
                  pl.BlockSpec((fc, D_MODEL), lambda bb, i, c: (c, 0)), vec, vec],
        out_specs=row_spec,
        out_shape=jax.ShapeDtypeStruct(x.shape, F32),
        scratch_shapes=[pltpu.VMEM((tm, D_MODEL), F32)],
        compiler_params=_params(("arbitrary", "arbitrary", "arbitrary")),
        name="ffn",
    )(x, mod, mod, mod, w_gu, w_gu, w_dn, g, b)


def _rope_tables(pos):
    half = HEAD_DIM // 2
    inv = ROPE_THETA ** (-jnp.arange(half, dtype=F32) / half)
    ang = pos.astype(F32)[:, None] * inv[None, :]
    cos = jnp.cos(ang)
    sin = jnp.sin(ang)
    return jnp.concatenate([cos] * 4, axis=1), jnp.concatenate([-sin, sin, -sin, sin], axis=1)


def kernel(x_prompt, x_sample, cache_sb_k, cache_sb_v, cache_diff_k, cache_diff_v, page_table, c_prompt, c_sample, w_ada, b_ada, w_in, w_out, lambda_q1, lambda_k1, lambda_q2, lambda_k2, subln_g, ln1_g, ln1_b, w_gate_up, w_down, ln2_g, ln2_b):
    n_batch, seq, _ = x_prompt.shape
    n_dec, n_new, _ = x_sample.shape
    n_pool = cache_sb_k.shape[0]
    lyr = 0

    n_c = n_batch + n_dec
    pad = (-n_c) % 8
    c_all = jnp.concatenate([c_prompt, c_sample, jnp.zeros((pad, D_MODEL), F32)], axis=0)
    mod = _ada(c_all, w_ada[lyr], b_ada[lyr][None, :])
    mod_p = mod[:n_batch].reshape(n_batch, 1, 6 * D_MODEL)
    mod_s = jnp.repeat(mod[n_batch:n_c], n_new, axis=0).reshape(1, n_dec * n_new, 6 * D_MODEL)

    w_in_b = w_in[lyr].astype(BF16)
    w_out_b = w_out[lyr].astype(BF16)
    w_gu_b = w_gate_up[lyr].astype(BF16)
    w_dn_b = w_down[lyr].astype(BF16)
    lams = (lambda_q1, lambda_k1, lambda_q2, lambda_k2)
    g1, b1, g2, b2 = ln1_g[lyr][None, :], ln1_b[lyr][None, :], ln2_g[lyr][None, :], ln2_b[lyr][None, :]

    cos_p, sin_p = _rope_tables(jnp.arange(seq, dtype=jnp.int32))
    qkv_p, ksb_p, vsb_p, kd_p, vd_p, vdt_p = _inproj(x_prompt, mod_p, w_in_b, cos_p, sin_p, tm=512)
    sb_o_p = _sb_attn(qkv_p, tq=256)
    d_o_p = _diff_attn(qkv_p, vdt_p, lams, subln_g, tq=256)
    x1_p = _merge(x_prompt, sb_o_p, d_o_p, mod_p, w_out_b, g1, b1, tm=512)
    y_p = _ffn(x1_p, mod_p, w_gu_b, w_dn_b, g2, b2, tm=512, n_chunks=2)

    rows_s = n_dec * n_new
    cos_n, sin_n = _rope_tables(PAST_LEN + jnp.arange(n_new, dtype=jnp.int32))
    cos_s = jnp.tile(cos_n, (n_dec, 1))
    sin_s = jnp.tile(sin_n, (n_dec, 1))
    xs = x_sample.reshape(1, rows_s, D_MODEL)
    qkv_s, ksb_s, vsb_s, kd_s, vd_s, _ = _inproj(xs, mod_s, w_in_b, cos_s, sin_s, tm=256)

    w = SB_WIDTH

    def block_diag_rows(q):
        q = q.reshape(n_dec, n_new, SB_HEADS, HEAD_DIM).transpose(0, 2, 1, 3)
        same = jnp.eye(SB_HEADS, dtype=bool)[None, :, None, :, None]
        return jnp.where(same, q[:, :, :, None, :], 0).reshape(n_dec, SB_HEADS * n_new, w)

    def token_minor(cache):
        return cache.transpose(0, 1, 3, 4, 2).reshape(n_pool, w, PAGE_SIZE)

    so, do = _decode(
        page_table, block_diag_rows(qkv_s[0, :, 0:w]), block_diag_rows(qkv_s[0, :, 3 * w:4 * w]),
        ksb_s.reshape(n_dec, n_new, w), vsb_s.reshape(n_dec, n_new, w),
        kd_s.reshape(n_dec, n_new, w), vd_s.reshape(n_dec, n_new, w),
        token_minor(cache_sb_k), token_minor(cache_sb_v), token_minor(cache_diff_k),
        cache_diff_v.reshape(n_pool, PAGE_SIZE * DIFF_HEADS, DIFF_V_DIM),
        lams, subln_g)
    sb_o_s = so.reshape(n_dec, SB_HEADS, n_new, HEAD_DIM).transpose(0, 2, 1, 3).reshape(1, rows_s, w)
    d_o_s = do.reshape(n_dec, DIFF_HEADS, n_new, DIFF_V_DIM).transpose(0, 2, 1, 3).reshape(1, rows_s, w)
    x1_s = _merge(xs, sb_o_s.astype(BF16), d_o_s.astype(BF16), mod_s, w_out_b, g1, b1, tm=256)
    y_s = _ffn(x1_s, mod_s, w_gu_b, w_dn_b, g2, b2, tm=256, n_chunks=2)

    return (y_p, y_s.reshape(n_dec, n_new, D_MODEL),
            ksb_p.reshape(n_batch, 1, seq, SB_HEADS, HEAD_DIM),
            vsb_p.reshape(n_batch, 1, seq, SB_HEADS, HEAD_DIM),
            kd_p.reshape(n_batch, 1, seq, 2 * DIFF_HEADS, HEAD_DIM),
            vd_p.reshape(n_batch, 1, seq, DIFF_HEADS, DIFF_V_DIM),
            ksb_s.reshape(n_dec, 1, n_new, SB_HEADS, HEAD_DIM),
            vsb_s.reshape(n_dec, 1, n_new, SB_HEADS, HEAD_DIM),
            kd_s.reshape(n_dec, 1, n_new, 2 * DIFF_HEADS, HEAD_DIM),
            vd_s.reshape(n_dec, 1, n_new, DIFF_HEADS, DIFF_V_DIM))
```
